```python
import math
import jax, jax.numpy as jnp
from jax import lax
import numpy as np

D_MODEL = 1024
BATCH = 2
SEQ = 16384
DEPTH = 4

N_MIXERS = 3
ALPHA = (2.0 * DEPTH) ** 0.25
BETA = (8.0 * DEPTH) ** -0.25
LN_EPS = 1e-5
RMS_EPS = 1e-6

SSD_D_INNER = 2 * D_MODEL
SSD_HEAD_DIM = 64
SSD_N_HEADS = SSD_D_INNER // SSD_HEAD_DIM
SSD_N_GROUPS = 8
SSD_HPG = SSD_N_HEADS // SSD_N_GROUPS
SSD_D_STATE = 128
SSD_CONV = 4
SSD_CHUNK = 128
SSD_CONV_DIM = SSD_D_INNER + 2 * SSD_N_GROUPS * SSD_D_STATE
SSD_IN_DIM = SSD_D_INNER + SSD_CONV_DIM + SSD_N_HEADS

MLA_N_HEADS = 16
MLA_NOPE = 64
MLA_ROPE = 32
MLA_V = 64
MLA_Q_RANK = 512
MLA_KV_RANK = 256
MLA_IN_DIM = MLA_Q_RANK + MLA_KV_RANK + MLA_ROPE
MLA_BLOCK = 128
ROPE_THETA = 10000.0

SG_D = 2 * D_MODEL
SG_GROUPS = 8
SG_GROUP_DIM = SG_D // SG_GROUPS
SG_CHUNK = 128

D_FF = 2816
N_EXPERTS = 8
TOP_K = 2

N_SSD = (DEPTH + 2) // 3
N_MLA = (DEPTH + 1) // 3
N_SG = DEPTH // 3
N_DENSE = (DEPTH + 1) // 2
N_MOE = DEPTH // 2

kernel_name = "hybrid_ssd_mla_gmlp_moe_deepnorm"


def layer_norm(x, g, b):
    xf = x.astype(jnp.float32)
    mu = jnp.mean(xf, -1, keepdims=True)
    var = jnp.mean(jnp.square(xf - mu), -1, keepdims=True)
    return ((xf - mu) * lax.rsqrt(var + LN_EPS)).astype(x.dtype) * g + b


def rms_norm(x, w):
    xf = x.astype(jnp.float32)
    return (xf * lax.rsqrt(jnp.mean(xf * xf, -1, keepdims=True) + RMS_EPS)).astype(x.dtype) * w


def rope(x, positions):
    half = x.shape[-1] // 2
    freqs = ROPE_THETA ** (-jnp.arange(half, dtype=jnp.float32) / half)
    ang = positions[..., None].astype(jnp.float32) * freqs
    cos = jnp.cos(ang)[:, :, None, :].astype(x.dtype)
    sin = jnp.sin(ang)[:, :, None, :].astype(x.dtype)
    x1, x2 = x[..., :half], x[..., half:]
    return jnp.concatenate([x1 * cos - x2 * sin, x2 * cos + x1 * sin], axis=-1)


def ssd_mixer(h, w_in, conv_w, conv_b, dt_bias, a_log, d_skip, norm_w, w_out):
    b, s, _ = h.shape
    G, R, P, N, L = SSD_N_GROUPS, SSD_HPG, SSD_HEAD_DIM, SSD_D_STATE, SSD_CHUNK
    z, xbc, dt = jnp.split(h @ w_in, [SSD_D_INNER, SSD_D_INNER + SSD_CONV_DIM], axis=-1)
    xbc = lax.conv_general_dilated(xbc, conv_w[:, None, :], window_strides=(1,),
                                   padding=[(SSD_CONV - 1, 0)],
                                   dimension_numbers=('NWC', 'WIO', 'NWC'),
                                   feature_group_count=SSD_CONV_DIM) + conv_b
    xbc = jax.nn.silu(xbc)
    xs, Bm, Cm = jnp.split(xbc, [SSD_D_INNER, SSD_D_INNER + G * N], axis=-1)
    dt = jax.nn.softplus(dt + dt_bias).astype(jnp.float32)
    A = -jnp.exp(a_log.astype(jnp.float32)).reshape(G, R)
    nc = s // L
    x = xs.reshape(b, nc, L, G, R, P)
    Bc = Bm.reshape(b, nc, L, G, N)
    Cc = Cm.reshape(b, nc, L, G, N)
    dt = dt.reshape(b, nc, L, G, R)
    a_cum = jnp.cumsum(dt * A, axis=2)
    xdt = x * dt[..., None]
    acum_t = jnp.transpose(a_cum, (0, 1, 3, 4, 2))
    seg = acum_t[..., :, None] - acum_t[..., None, :]
    causal = jnp.tril(jnp.ones((L, L), dtype=bool))
    decay = jnp.exp(jnp.where(causal, seg, -jnp.inf))
    cb = jnp.einsum('bclgn,bcsgn->bcgls', Cc, Bc)
    y_diag = jnp.einsum('bcgls,bcgrls,bcsgrp->bclgrp', cb, decay, xdt)
    decay_to_end = jnp.exp(a_cum[:, :, -1:] - a_cum)
    states = jnp.einsum('bclgn,bclgr,bclgrp->bcgrpn', Bc, decay_to_end, xdt)
    chunk_decay = jnp.exp(a_cum[:, :, -1])

    def step(carry, inp):
        st, dec = inp
        return carry * dec[..., None, None] + st, carry

    init = jnp.zeros((b, G, R, P, N), states.dtype)
    _, prev = lax.scan(step, init, (jnp.moveaxis(states, 1, 0), jnp.moveaxis(chunk_decay, 1, 0)))
    prev = jnp.moveaxis(prev, 0, 1)
    y_off = jnp.einsum('bclgn,bcgrpn,bclgr->bclgrp', Cc, prev, jnp.exp(a_cum))
    y = y_diag + y_off + x * d_skip.reshape(G, R)[..., None]
    y = y.reshape(b, s, SSD_D_INNER)
    yg = (y * jax.nn.silu(z)).reshape(b, s, G, SSD_D_INNER // G)
    yg = rms_norm(yg, norm_w.reshape(G, SSD_D_INNER // G)).reshape(b, s, SSD_D_INNER)
    return (yg @ w_out).astype(h.dtype)


def mla_mixer(h, positions, w_in, q_norm, kv_norm, w_uq, w_ukv, w_out):
    b, s, _ = h.shape
    H = MLA_N_HEADS
    cq, ckv, k_rope = jnp.split(h @ w_in, [MLA_Q_RANK, MLA_Q_RANK + MLA_KV_RANK], axis=-1)
    q = (rms_norm(cq, q_norm) @ w_uq).reshape(b, s, H, MLA_NOPE + MLA_ROPE)
    q_nope = q[..., :MLA_NOPE]
    q_rope = rope(q[..., MLA_NOPE:], positions)
    kv = (rms_norm(ckv, kv_norm) @ w_ukv).reshape(b, s, H, MLA_NOPE + MLA_V)
    k_nope, v = kv[..., :MLA_NOPE], kv[..., MLA_NOPE:]
    k_rope = rope(k_rope[:, :, None, :], positions)[:, :, 0]
    scale = (MLA_NOPE + MLA_ROPE) ** -0.5
    nb = s // MLA_BLOCK
    qn_blocks = jnp.moveaxis(q_nope.reshape(b, nb, MLA_BLOCK, H, MLA_NOPE), 1, 0)
    qr_blocks = jnp.moveaxis(q_rope.reshape(b, nb, MLA_BLOCK, H, MLA_ROPE), 1, 0)
    key_idx = jnp.arange(s)

    def attend(args):
        qn, qr, i = args
        scores = (jnp.einsum('bqhd,bkhd->bhqk', qn, k_nope)
                  + jnp.einsum('bqhd,bkd->bhqk', qr, k_rope)).astype(jnp.float32) * scale
        q_idx = i * MLA_BLOCK + jnp.arange(MLA_BLOCK)
        scores = jnp.where(key_idx[None, :] <= q_idx[:, None], scores, -jnp.inf)
        p = jax.nn.softmax(scores, axis=-1).astype(v.dtype)
        return jnp.einsum('bhqk,bkhd->bqhd', p, v)

    out = lax.map(attend, (qn_blocks, qr_blocks, jnp.arange(nb)))
    out = jnp.moveaxis(out, 0, 1).reshape(b, s, H * MLA_V)
    return (out @ w_out).astype(h.dtype)


def sg_mixer(h, w_in, b_in, ln_g, ln_b, w_s, b_s, w_out):
    b, s, _ = h.shape
    u, v = jnp.split(jax.nn.gelu(h @ w_in + b_in), 2, axis=-1)
    v = layer_norm(v, ln_g, ln_b)
    nc = s // SG_CHUNK
    v = v.reshape(b, nc, SG_CHUNK, SG_GROUPS, SG_GROUP_DIM)
    causal = jnp.tril(jnp.ones((SG_CHUNK, SG_CHUNK), dtype=w_s.dtype))
    mixed = jnp.einsum('gts,bcsgd->bctgd', w_s * causal, v) + b_s.T[:, :, None]
    return ((u * mixed.reshape(b, s, SG_D)) @ w_out).astype(h.dtype)


def swiglu(h, w_gate, w_up, w_down):
    return (jax.nn.silu(h @ w_gate) * (h @ w_up)) @ w_down


def moe(h, w_router, w_gate, w_up, w_down):
    logits = (h @ w_router).astype(jnp.float32)
    top_vals, top_idx = lax.top_k(logits, TOP_K)
    weights = jax.nn.softmax(top_vals, axis=-1)
    combine = jnp.sum(jax.nn.one_hot(top_idx, N_EXPERTS, dtype=jnp.float32) * weights[..., None], axis=-2)
    combine = combine.astype(h.dtype)
    out = combine[..., 0:1] * swiglu(h, w_gate[0], w_up[0], w_down[0])
    for e in range(1, N_EXPERTS):
        out = out + combine[..., e:e + 1] * swiglu(h, w_gate[e], w_up[e], w_down[e])
    return out


def setup_inputs(seed: int = 0) -> dict:
    key = jax.random.key(seed)
    ks = iter(jax.random.split(key, 48))

    def nrm(shape, scale):
        return jax.random.normal(next(ks), shape, jnp.float32) * scale

    D = D_MODEL
    x = nrm((BATCH, SEQ, D), 1.0)
    c = nrm((BATCH, D), 1.0)
    offset = jax.random.randint(next(ks), (BATCH, 1), 0, 4096, dtype=jnp.int32)
    positions = offset + jnp.arange(SEQ, dtype=jnp.int32)[None, :]
    ada_w = nrm((DEPTH, D, 6 * D), 0.1 * D ** -0.5)
    ada_b = nrm((DEPTH, 6 * D), 0.01)
    ln_g = 1.0 + nrm((DEPTH, 2, D), 0.02)
    ln_b = nrm((DEPTH, 2, D), 0.02)
    ssd_w_in = nrm((N_SSD, D, SSD_IN_DIM), D ** -0.5)
    ssd_conv_w = nrm((N_SSD, SSD_CONV, SSD_CONV_DIM), SSD_CONV ** -0.5)
    ssd_conv_b = nrm((N_SSD, SSD_CONV_DIM), 0.02)
    dt0 = jnp.exp(jax.random.uniform(next(ks), (N_SSD, SSD_N_HEADS), jnp.float32)
                  * (math.log(0.1) - math.log(1e-3)) + math.log(1e-3))
    ssd_dt_bias = dt0 + jnp.log(-jnp.expm1(-dt0))
    ssd_a_log = jnp.log(jax.random.uniform(next(ks), (N_SSD, SSD_N_HEADS), jnp.float32, 1.0, 16.0))
    ssd_d_skip = 1.0 + nrm((N_SSD, SSD_N_HEADS), 0.02)
    ssd_norm_w = 1.0 + nrm((N_SSD, SSD_D_INNER), 0.02)
    ssd_w_out = nrm((N_SSD, SSD_D_INNER, D), BETA * SSD_D_INNER ** -0.5)
    mla_w_in = nrm((N_MLA, D, MLA_IN_DIM), D ** -0.5)
    mla_q_norm = 1.0 + nrm((N_MLA, MLA_Q_RANK), 0.02)
    mla_kv_norm = 1.0 + nrm((N_MLA, MLA_KV_RANK), 0.02)
    mla_w_uq = nrm((N_MLA, MLA_Q_RANK, MLA_N_HEADS * (MLA_NOPE + MLA_ROPE)), MLA_Q_RANK ** -0.5)
    mla_w_ukv = nrm((N_MLA, MLA_KV_RANK, MLA_N_HEADS * (MLA_NOPE + MLA_V)), MLA_KV_RANK ** -0.5)
    mla_w_out = nrm((N_MLA, MLA_N_HEADS * MLA_V, D), BETA * (MLA_N_HEADS * MLA_V) ** -0.5)
    sg_w_in = nrm((N_SG, D, 2 * SG_D), D ** -0.5)
    sg_b_in = nrm((N_SG, 2 * SG_D), 0.02)
    sg_ln_g = 1.0 + nrm((N_SG, SG_D), 0.02)
    sg_ln_b = nrm((N_SG, SG_D), 0.02)
    sg_w_s = nrm((N_SG, SG_GROUPS, SG_CHUNK, SG_CHUNK), SG_CHUNK ** -0.5)
    sg_b_s = 1.0 + nrm((N_SG, SG_GROUPS, SG_CHUNK), 0.02)
    sg_w_out = nrm((N_SG, SG_D, D), BETA * SG_D ** -0.5)
    ffn_w_gate = nrm((N_DENSE, D, D_FF), D ** -0.5)
    ffn_w_up = nrm((N_DENSE, D, D_FF), D ** -0.5)
    ffn_w_down = nrm((N_DENSE, D_FF, D), BETA * D_FF ** -0.5)
    moe_w_router = nrm((N_MOE, D, N_EXPERTS), D ** -0.5)
    moe_w_gate = nrm((N_MOE, N_EXPERTS, D, D_FF), D ** -0.5)
    moe_w_up = nrm((N_MOE, N_EXPERTS, D, D_FF), D ** -0.5)
    moe_w_down = nrm((N_MOE, N_EXPERTS, D_FF, D), BETA * D_FF ** -0.5)
    return {
        'x': x, 'c': c, 'positions': positions,
        'ada_w': ada_w, 'ada_b': ada_b, 'ln_g': ln_g, 'ln_b': ln_b,
        'ssd_w_in': ssd_w_in, 'ssd_conv_w': ssd_conv_w, 'ssd_conv_b': ssd_conv_b,
        'ssd_dt_bias': ssd_dt_bias, 'ssd_a_log': ssd_a_log, 'ssd_d_skip': ssd_d_skip,
        'ssd_norm_w': ssd_norm_w, 'ssd_w_out': ssd_w_out,
        'mla_w_in': mla_w_in, 'mla_q_norm': mla_q_norm, 'mla_kv_norm': mla_kv_norm,
        'mla_w_uq': mla_w_uq, 'mla_w_ukv': mla_w_ukv, 'mla_w_out': mla_w_out,
        'sg_w_in': sg_w_in, 'sg_b_in': sg_b_in, 'sg_ln_g': sg_ln_g, 'sg_ln_b': sg_ln_b,
        'sg_w_s': sg_w_s, 'sg_b_s': sg_b_s, 'sg_w_out': sg_w_out,
        'ffn_w_gate': ffn_w_gate, 'ffn_w_up': ffn_w_up, 'ffn_w_down': ffn_w_down,
        'moe_w_router': moe_w_router, 'moe_w_gate': moe_w_gate, 'moe_w_up': moe_w_up,
        'moe_w_down': moe_w_down,
    }


def reference(x, c, positions, ada_w, ada_b, ln_g, ln_b,
              ssd_w_in, ssd_conv_w, ssd_conv_b, ssd_dt_bias, ssd_a_log, ssd_d_skip,
              ssd_norm_w, ssd_w_out,
              mla_w_in, mla_q_norm, mla_kv_norm, mla_w_uq, mla_w_ukv, mla_w_out,
              sg_w_in, sg_b_in, sg_ln_g, sg_ln_b, sg_w_s, sg_b_s, sg_w_out,
              ffn_w_gate, ffn_w_up, ffn_w_down,
              moe_w_router, moe_w_gate, moe_w_up, moe_w_down):
    cond = jax.nn.silu(c)
    for i in range(DEPTH):
        mod = cond @ ada_w[i] + ada_b[i]
        sh_m, sc_m, g_m, sh_f, sc_f, g_f = [m[:, None, :] for m in jnp.split(mod, 6, axis=-1)]
        hm = x * (1.0 + sc_m) + sh_m
        kind, j = i % N_MIXERS, i // N_MIXERS
        if kind == 0:
            y = ssd_mixer(hm, ssd_w_in[j], ssd_conv_w[j], ssd_conv_b[j], ssd_dt_bias[j],
                          ssd_a_log[j], ssd_d_skip[j], ssd_norm_w[j], ssd_w_out[j])
        elif kind == 1:
            y = mla_mixer(hm, positions, mla_w_in[j], mla_q_norm[j], mla_kv_norm[j],
                          mla_w_uq[j], mla_w_ukv[j], mla_w_out[j])
        else:
            y = sg_mixer(hm, sg_w_in[j], sg_b_in[j], sg_ln_g[j], sg_ln_b[j],
                         sg_w_s[j], sg_b_s[j], sg_w_out[j])
        x = layer_norm(ALPHA * x + (1.0 + g_m) * y, ln_g[i, 0], ln_b[i, 0])
        hf = x * (1.0 + sc_f) + sh_f
        k = i // 2
        if i % 2 == 0:
            y = swiglu(hf, ffn_w_gate[k], ffn_w_up[k], ffn_w_down[k])
        else:
            y = moe(hf, moe_w_router[k], moe_w_gate[k], moe_w_up[k], moe_w_down[k])
        x = layer_norm(ALPHA * x + (1.0 + g_f) * y, ln_g[i, 1], ln_b[i, 1])
    return x
```

```python
import functools
import math

import jax
import jax.numpy as jnp
from jax import lax
from jax.experimental import pallas as pl
from jax.experimental.pallas import tpu as pltpu

F32 = jnp.float32
BF16 = jnp.bfloat16
HIGHEST = lax.Precision.HIGHEST

DEPTH = 4
ALPHA = (2.0 * DEPTH) ** 0.25
LN_EPS = 1e-5
RMS_EPS = 1e-6
ROPE_THETA = 10000.0

LANES = 128
SUBLANES = 8
VMEM_LIMIT = 56 * 1024 * 1024

SSD_HEAD_DIM = 64
SSD_N_GROUPS = 8
SSD_HPG = 4
SSD_D_STATE = 128
SSD_CONV = 4
SSD_CHUNK = 128
SSD_GROUP_W = SSD_HPG * SSD_HEAD_DIM

MLA_N_HEADS = 16
MLA_NOPE = 64
MLA_ROPE = 32
MLA_V = 64
MLA_Q_RANK = 512
MLA_KV_RANK = 256
MLA_HEAD_PAD = 128
MLA_IN_PAD = MLA_Q_RANK + MLA_KV_RANK + MLA_HEAD_PAD
QK_SCALE = (MLA_NOPE + MLA_ROPE) ** -0.5
LOG2E = 1.4426950408889634
NEG_BIG = -1e30

SG_GROUPS = 8
SG_CHUNK = 128

N_EXPERTS = 8


def _cparams(*sem):
    return pltpu.CompilerParams(dimension_semantics=sem, vmem_limit_bytes=VMEM_LIMIT)


def _layer_norm(r):
    mu = jnp.mean(r, axis=-1, keepdims=True)
    d = r - mu
    var = jnp.mean(d * d, axis=-1, keepdims=True)
    return d * lax.rsqrt(var + LN_EPS)


def _silu(v):
    return v * jax.nn.sigmoid(v)


def _modulate(x_ref, sc_ref, sh_ref):
    return x_ref[0] * (1.0 + sc_ref[0]) + sh_ref[0]


def _res_ln(x_ref, gate_ref, y, lng_ref, lnb_ref):
    r = ALPHA * x_ref[0] + (1.0 + gate_ref[0]) * y
    return _layer_norm(r) * lng_ref[...] + lnb_ref[...]


def _ada_kernel(c_ref, w_ref, b_ref, o_ref):
    cond = _silu(c_ref[...])
    o_ref[0, 0] = jnp.dot(cond, w_ref[0], preferred_element_type=F32, precision=HIGHEST) + b_ref[0, 0]


def _ada_mod(c_pad, ada_w, ada_b):
    depth, d, _ = ada_w.shape
    rows = c_pad.shape[0]
    return pl.pallas_call(
        _ada_kernel,
        grid=(depth, 6),
        in_specs=[
            pl.BlockSpec((rows, d), lambda i, j: (0, 0)),
            pl.BlockSpec((1, d, d), lambda i, j: (i, 0, j)),
            pl.BlockSpec((1, 1, 1, d), lambda i, j: (i, j, 0, 0)),
        ],
        out_specs=pl.BlockSpec((1, 1, rows, d), lambda i, j: (i, j, 0, 0)),
        out_shape=jax.ShapeDtypeStruct((depth, 6, rows, d), F32),
        compiler_params=_cparams("arbitrary", "arbitrary"),
        name="ada_mod",
    )(c_pad, ada_w, ada_b.reshape(depth, 6, 1, d))


def _modmm_kernel(x_ref, sc_ref, sh_ref, w_ref, o_ref, h_ref):
    @pl.when(pl.program_id(2) == 0)
    def _():
        h_ref[...] = _modulate(x_ref, sc_ref, sh_ref).astype(BF16)

    o_ref[0] = jnp.dot(h_ref[...], w_ref[...], preferred_element_type=F32).astype(o_ref.dtype)


def _mod_matmul(x, sc, sh, w, *, tm, tn, out_dtype, name):
    b, s, d = x.shape
    n = w.shape[1]
    return pl.pallas_call(
        _modmm_kernel,
        grid=(b, s // tm, n // tn),
        in_specs=[
            pl.BlockSpec((1, tm, d), lambda bi, i, j: (bi, i, 0)),
            pl.BlockSpec((1, 1, d), lambda bi, i, j: (bi, 0, 0)),
            pl.BlockSpec((1, 1, d), lambda bi, i, j: (bi, 0, 0)),
            pl.BlockSpec((d, tn), lambda bi, i, j: (0, j)),
        ],
        out_specs=pl.BlockSpec((1, tm, tn), lambda bi, i, j: (bi, i, j)),
        out_shape=jax.ShapeDtypeStruct((b, s, n), out_dtype),
        scratch_shapes=[pltpu.VMEM((tm, d), BF16)],
        compiler_params=_cparams("parallel", "parallel", "arbitrary"),
        name=name,
    )(x, sc, sh, w)


def _ssd_in_kernel(x_ref, sc_ref, sh_ref, w_ref, wdt_ref, o_ref, dt_ref, h_ref):
    @pl.when(pl.program_id(2) == 0)
    def _():
        h_ref[...] = _modulate(x_ref, sc_ref, sh_ref).astype(BF16)
        dt_ref[0] = jnp.dot(h_ref[...], wdt_ref[...], preferred_element_type=F32)

    o_ref[0] = jnp.dot(h_ref[...], w_ref[...], preferred_element_type=F32).astype(o_ref.dtype)


def _ssd_in_proj(x, sc, sh, w_zxbc, w_dt, *, tm, tn):
    b, s, d = x.shape
    n = w_zxbc.shape[1]
    return pl.pallas_call(
        _ssd_in_kernel,
        grid=(b, s // tm, n // tn),
        in_specs=[
            pl.BlockSpec((1, tm, d), lambda bi, i, j: (bi, i, 0)),
            pl.BlockSpec((1, 1, d), lambda bi, i, j: (bi, 0, 0)),
            pl.BlockSpec((1, 1, d), lambda bi, i, j: (bi, 0, 0)),
            pl.BlockSpec((d, tn), lambda bi, i, j: (0, j)),
            pl.BlockSpec((d, LANES), lambda bi, i, j: (0, 0)),
        ],
        out_specs=[
            pl.BlockSpec((1, tm, tn), lambda bi, i, j: (bi, i, j)),
            pl.BlockSpec((1, tm, LANES), lambda bi, i, j: (bi, i, 0)),
        ],
        out_shape=[
            jax.ShapeDtypeStruct((b, s, n), BF16),
            jax.ShapeDtypeStruct((b, s, LANES), F32),
        ],
        scratch_shapes=[pltpu.VMEM((tm, d), BF16)],
        compiler_params=_cparams("parallel", "parallel", "arbitrary"),
        name="ssd_in_proj",
    )(x, sc, sh, w_zxbc, w_dt)


def _ssd_kernel(z_ref, x_ref, bc_ref, dt_ref, cwx_ref, cwbc_ref, cbx_ref, cbbc_ref, dtb_ref, alog_ref,
                dskip_ref, nw_ref, o_ref, extx_ref, extbc_ref, xs_ref, b_ref, c_ref, state_ref):
    L = SSD_CHUNK
    gw = SSD_GROUP_W
    n_st = SSD_D_STATE
    tail = SUBLANES
    ci = pl.program_id(1)

    @pl.when(ci == 0)
    def _():
        extx_ref[0:tail] = jnp.zeros((tail, extx_ref.shape[1]), F32)
        extbc_ref[0:tail] = jnp.zeros((tail, extbc_ref.shape[1]), F32)
        state_ref[...] = jnp.zeros(state_ref.shape, F32)

    @pl.when(ci != 0)
    def _():
        extx_ref[0:tail] = extx_ref[L:L + tail]
        extbc_ref[0:tail] = extbc_ref[L:L + tail]

    extx_ref[tail:tail + L] = x_ref[0].astype(F32)
    extbc_ref[tail:tail + L] = bc_ref[0].astype(F32)

    def conv(ext_ref, cw_ref, cb_ref):
        acc = cb_ref[...]
        for k in range(SSD_CONV):
            off = tail - (SSD_CONV - 1) + k
            acc = acc + ext_ref[off:off + L, :] * cw_ref[k:k + 1, :]
        return _silu(acc)

    xs_ref[...] = conv(extx_ref, cwx_ref, cbx_ref)
    bc = conv(extbc_ref, cwbc_ref, cbbc_ref)
    half = bc.shape[1] // 2
    b_ref[...] = bc[:, :half]
    c_ref[...] = bc[:, half:].astype(BF16)

    dtv = dt_ref[0] + dtb_ref[...]
    dt = jnp.maximum(dtv, 0.0) + jnp.log1p(jnp.exp(-jnp.abs(dtv)))
    a_neg = -jnp.exp(alog_ref[...])
    row = lax.broadcasted_iota(jnp.int32, (L, L), 0)
    col = lax.broadcasted_iota(jnp.int32, (L, L), 1)
    causal = col <= row
    tri = jnp.where(causal, 1.0, 0.0).astype(F32)
    a_cum = jnp.dot(tri, dt * a_neg, preferred_element_type=F32, precision=HIGHEST)
    a_cum_t = a_cum.T
    ea = jnp.exp(a_cum)
    dte = jnp.exp(a_cum[L - 1:L, :] - a_cum)

    lane = lax.broadcasted_iota(jnp.int32, (L, LANES), 1)
    head_of_lane = lax.broadcasted_iota(jnp.int32, (L, gw), 1) // SSD_HEAD_DIM

    def expand(v, g):
        def pair(h0, h1):
            return jnp.where(lane < SSD_HEAD_DIM,
                             jnp.broadcast_to(v[:, h0:h0 + 1], (L, LANES)),
                             jnp.broadcast_to(v[:, h1:h1 + 1], (L, LANES)))
        h = SSD_HPG * g
        return jnp.concatenate([pair(h, h + 1), pair(h + 2, h + 3)], axis=1)

    for g in range(SSD_N_GROUPS):
        sl = slice(g * gw, (g + 1) * gw)
        sn = slice(g * n_st, (g + 1) * n_st)
        c_g = c_ref[:, sn]
        b_g32 = b_ref[:, sn]
        b_g = b_g32.astype(BF16)
        b_gt = b_g32.T.astype(BF16)
        cb = lax.dot_general(c_g, b_g, (((1,), (1,)), ((), ())), preferred_element_type=F32)
        st = state_ref[g]
        y_off = jnp.dot(c_g, st.astype(BF16), preferred_element_type=F32)
        xs_g = xs_ref[:, sl]
        dt_g = expand(dt, g)
        dte_g = expand(dte, g)
        ea_g = expand(ea, g)
        xdt = xs_g * dt_g
        y_diag = jnp.zeros((L, gw), F32)
        for r in range(SSD_HPG):
            h = SSD_HPG * g + r
            seg = a_cum[:, h:h + 1] - a_cum_t[h:h + 1, :]
            decay = jnp.exp(jnp.where(causal, seg, NEG_BIG))
            m = (cb * decay).astype(BF16)
            x_r = jnp.where(head_of_lane == r, xdt, 0.0).astype(BF16)
            y_diag = y_diag + jnp.dot(m, x_r, preferred_element_type=F32)
        y = y_diag + y_off * ea_g + xs_g * dskip_ref[:, sl]
        z_g = z_ref[0, :, sl].astype(F32)
        t = y * _silu(z_g)
        ms = jnp.mean(t * t, axis=-1, keepdims=True)
        o_ref[0, :, sl] = (t * lax.rsqrt(ms + RMS_EPS) * nw_ref[:, sl]).astype(o_ref.dtype)
        new_st = jnp.dot(b_gt, (xdt * dte_g).astype(BF16), preferred_element_type=F32)
        state_ref[g] = st * ea_g[L - 1:L, :] + new_st


def _ssd_scan(zxbc, dt_raw, conv_w, conv_b, dt_bias, a_log, d_skip, norm_w):
    b, s, _ = zxbc.shape
    L = SSD_CHUNK
    di = SSD_N_GROUPS * SSD_GROUP_W
    blk = lambda j: pl.BlockSpec((1, L, di), lambda bi, ci, j=j: (bi, ci, j))
    vec = lambda w: pl.BlockSpec((1, w), lambda bi, ci: (0, 0))
    cw = lambda j: pl.BlockSpec((SSD_CONV, di), lambda bi, ci, j=j: (0, j))
    cb = lambda j: pl.BlockSpec((1, di), lambda bi, ci, j=j: (0, j))
    return pl.pallas_call(
        _ssd_kernel,
        grid=(b, s // L),
        in_specs=[
            blk(0), blk(1), blk(2),
            pl.BlockSpec((1, L, LANES), lambda bi, ci: (bi, ci, 0)),
            cw(0), cw(1), cb(0), cb(1),
            vec(LANES), vec(LANES), vec(di), vec(di),
        ],
        out_specs=pl.BlockSpec((1, L, di), lambda bi, ci: (bi, ci, 0)),
        out_shape=jax.ShapeDtypeStruct((b, s, di), BF16),
        scratch_shapes=[
            pltpu.VMEM((L + SUBLANES, di), F32),
            pltpu.VMEM((L + SUBLANES, di), F32),
            pltpu.VMEM((L, di), F32),
            pltpu.VMEM((L, di // 2), F32),
            pltpu.VMEM((L, di // 2), BF16),
            pltpu.VMEM((SSD_N_GROUPS, SSD_D_STATE, SSD_GROUP_W), F32),
        ],
        compiler_params=_cparams("arbitrary", "arbitrary"),
        name="ssd_scan",
    )(zxbc, zxbc, zxbc, dt_raw, conv_w, conv_w, conv_b, conv_b, dt_bias, a_log, d_skip, norm_w)


def _mm_res_ln_kernel(a_ref, w_ref, x_ref, gate_ref, lng_ref, lnb_ref, o_ref):
    y = jnp.dot(a_ref[0], w_ref[...], preferred_element_type=F32)
    o_ref[0] = _res_ln(x_ref, gate_ref, y, lng_ref, lnb_ref)


def _mm_res_ln(a, w, x, gate, lng, lnb, *, tm, name):
    b, s, d = x.shape
    k = a.shape[2]
    return pl.pallas_call(
        _mm_res_ln_kernel,
        grid=(b, s // tm),
        in_specs=[
            pl.BlockSpec((1, tm, k), lambda bi, i: (bi, i, 0)),
            pl.BlockSpec((k, d), lambda bi, i: (0, 0)),
            pl.BlockSpec((1, tm, d), lambda bi, i: (bi, i, 0)),
            pl.BlockSpec((1, 1, d), lambda bi, i: (bi, 0, 0)),
            pl.BlockSpec((1, d), lambda bi, i: (0, 0)),
            pl.BlockSpec((1, d), lambda bi, i: (0, 0)),
        ],
        out_specs=pl.BlockSpec((1, tm, d), lambda bi, i: (bi, i, 0)),
        out_shape=jax.ShapeDtypeStruct((b, s, d), F32),
        compiler_params=_cparams("parallel", "parallel"),
        name=name,
    )(a, w, x, gate, lng, lnb)


def _mla_proj_kernel(cin_ref, pos_ref, freq_ref, sign_ref, qn_ref, kvn_ref, wuq_ref, wuk_ref, wuv_ref,
                     q_ref, k_ref, v_ref):
    cin = cin_ref[0]
    tm = cin.shape[0]
    hp = MLA_HEAD_PAD

    def rms(v, w_ref):
        return v * lax.rsqrt(jnp.mean(v * v, axis=-1, keepdims=True) + RMS_EPS) * w_ref[...]

    cq = rms(cin[:, :MLA_Q_RANK], qn_ref).astype(BF16)
    ckv = rms(cin[:, MLA_Q_RANK:MLA_Q_RANK + MLA_KV_RANK], kvn_ref).astype(BF16)
    k_rope = cin[:, MLA_Q_RANK + MLA_KV_RANK:]
    q = jnp.dot(cq, wuq_ref[...], preferred_element_type=F32)
    k = jnp.dot(ckv, wuk_ref[...], preferred_element_type=F32)
    v_ref[0] = jnp.dot(ckv, wuv_ref[...], preferred_element_type=F32).astype(v_ref.dtype)

    ang = pos_ref[0].astype(F32) * freq_ref[...]
    cos = jnp.cos(ang)
    sin = jnp.sin(ang) * sign_ref[...]
    lane = lax.broadcasted_iota(jnp.int32, (tm, hp), 1)
    first_half = lane < MLA_NOPE + MLA_ROPE // 2

    def rope(xh):
        swapped = jnp.where(first_half,
                            pltpu.roll(xh, hp - MLA_ROPE // 2, 1),
                            pltpu.roll(xh, MLA_ROPE // 2, 1))
        return xh * cos + swapped * sin

    kr = rope(k_rope)
    for h in range(MLA_N_HEADS):
        sl = slice(h * hp, (h + 1) * hp)
        q_ref[0, :, sl] = (rope(q[:, sl]) * (QK_SCALE * LOG2E)).astype(q_ref.dtype)
        k_ref[0, :, sl] = (k[:, sl] + kr).astype(k_ref.dtype)


def _mla_proj(cin, pos, freq, sign, q_norm, kv_norm, w_uq, w_uk, w_uv, *, tm):
    b, s, n_in = cin.shape
    nq = w_uq.shape[1]
    nv = w_uv.shape[1]
    full = lambda a: pl.BlockSpec(a.shape, lambda bi, i: (0,) * a.ndim)
    return pl.pallas_call(
        _mla_proj_kernel,
        grid=(b, s // tm),
        in_specs=[
            pl.BlockSpec((1, tm, n_in), lambda bi, i: (bi, i, 0)),
            pl.BlockSpec((1, tm, 1), lambda bi, i: (bi, i, 0)),
            full(freq), full(sign), full(q_norm), full(kv_norm), full(w_uq), full(w_uk), full(w_uv),
        ],
        out_specs=[
            pl.BlockSpec((1, tm, nq), lambda bi, i: (bi, i, 0)),
            pl.BlockSpec((1, tm, nq), lambda bi, i: (bi, i, 0)),
            pl.BlockSpec((1, tm, nv), lambda bi, i: (bi, i, 0)),
        ],
        out_shape=[
            jax.ShapeDtypeStruct((b, s, nq), BF16),
            jax.ShapeDtypeStruct((b, s, nq), BF16),
            jax.ShapeDtypeStruct((b, s, nv), BF16),
        ],
        compiler_params=_cparams("parallel", "parallel"),
        name="mla_proj",
    )(cin, pos, freq, sign, q_norm, kv_norm, w_uq, w_uk, w_uv)


def _attn_kernel(q_ref, k_ref, v_ref, o_ref, acc_ref, *, tq):
    qi = pl.program_id(2)
    hp = MLA_HEAD_PAD
    row = lax.broadcasted_iota(jnp.int32, (tq, tq), 0)
    col = lax.broadcasted_iota(jnp.int32, (tq, tq), 1)
    outs = []
    for hh in range(2):
        q = q_ref[0, :, hh * hp:(hh + 1) * hp]

        def step(kb, carry, masked, hh=hh, q=q):
            m_prev, l_prev = carry
            start = pl.multiple_of(kb * tq, tq)
            k = k_ref[0, pl.ds(start, tq), hh * hp:(hh + 1) * hp]
            v = v_ref[0, pl.ds(start, tq), :]
            s = lax.dot_general(q, k, (((1,), (1,)), ((), ())), preferred_element_type=F32)
            if masked:
                s = jnp.where(col <= row, s, NEG_BIG)
            m_new = jnp.maximum(m_prev, jnp.max(s, axis=1, keepdims=True))
            alpha = jnp.exp2(m_prev - m_new)
            p = jnp.exp2(s - m_new)
            l_new = alpha * l_prev + jnp.sum(p, axis=1, keepdims=True)
            acc_ref[hh] = acc_ref[hh] * alpha + jnp.dot(p.astype(BF16), v, preferred_element_type=F32)
            return m_new, l_new

        acc_ref[hh] = jnp.zeros((tq, hp), F32)
        carry = (jnp.full((tq, 1), NEG_BIG, F32), jnp.zeros((tq, 1), F32))
        carry = lax.fori_loop(0, qi, functools.partial(step, masked=False), carry)
        _, l_fin = step(qi, carry, masked=True)
        outs.append(acc_ref[hh] / l_fin)
    lane = lax.broadcasted_iota(jnp.int32, (tq, hp), 1)
    o_ref[0] = jnp.where(lane < MLA_V, outs[0], outs[1]).astype(o_ref.dtype)


def _attention(q, k, v, *, tq):
    b, s, _ = q.shape
    pairs = MLA_N_HEADS // 2
    return pl.pallas_call(
        functools.partial(_attn_kernel, tq=tq),
        grid=(b, pairs, s // tq),
        in_specs=[
            pl.BlockSpec((1, tq, 2 * MLA_HEAD_PAD), lambda bi, j, i: (bi, i, j)),
            pl.BlockSpec((1, s, 2 * MLA_HEAD_PAD), lambda bi, j, i: (bi, 0, j)),
            pl.BlockSpec((1, s, 2 * MLA_V), lambda bi, j, i: (bi, 0, j)),
        ],
        out_specs=pl.BlockSpec((1, tq, 2 * MLA_V), lambda bi, j, i: (bi, i, j)),
        out_shape=jax.ShapeDtypeStruct((b, s, MLA_N_HEADS * MLA_V), BF16),
        scratch_shapes=[pltpu.VMEM((2, tq, MLA_HEAD_PAD), F32)],
        compiler_params=_cparams("parallel", "parallel", "arbitrary"),
        name="mla_attention",
    )(q, k, v)


def _sg_in_kernel(x_ref, sc_ref, sh_ref, w_ref, b_ref, lng_ref, lnb_ref, o_ref, h_ref):
    j = pl.program_id(2)

    @pl.when(j == 0)
    def _():
        h_ref[...] = _modulate(x_ref, sc_ref, sh_ref).astype(BF16)

    y = jax.nn.gelu(jnp.dot(h_ref[...], w_ref[...], preferred_element_type=F32) + b_ref[...])

    @pl.when(j == 0)
    def _():
        o_ref[0] = y.astype(o_ref.dtype)

    @pl.when(j == 1)
    def _():
        o_ref[0] = (_layer_norm(y) * lng_ref[...] + lnb_ref[...]).astype(o_ref.dtype)


def _sg_in_proj(x, sc, sh, w, bias, lng, lnb, *, tm):
    b, s, d = x.shape
    n = w.shape[1]
    tn = n // 2
    return pl.pallas_call(
        _sg_in_kernel,
        grid=(b, s // tm, 2),
        in_specs=[
            pl.BlockSpec((1, tm, d), lambda bi, i, j: (bi, i, 0)),
            pl.BlockSpec((1, 1, d), lambda bi, i, j: (bi, 0, 0)),
            pl.BlockSpec((1, 1, d), lambda bi, i, j: (bi, 0, 0)),
            pl.BlockSpec((d, tn), lambda bi, i, j: (0, j)),
            pl.BlockSpec((1, tn), lambda bi, i, j: (0, j)),
            pl.BlockSpec((1, tn), lambda bi, i, j: (0, 0)),
            pl.BlockSpec((1, tn), lambda bi, i, j: (0, 0)),
        ],
        out_specs=pl.BlockSpec((1, tm, tn), lambda bi, i, j: (bi, i, j)),
        out_shape=jax.ShapeDtypeStruct((b, s, n), BF16),
        scratch_shapes=[pltpu.VMEM((tm, d), BF16)],
        compiler_params=_cparams("parallel", "parallel", "arbitrary"),
        name="sg_in_proj",
    )(x, sc, sh, w, bias, lng, lnb)


def _sg_out_kernel(u_ref, v_ref, ws_ref, bst_ref, w_ref, x_ref, gate_ref, lng_ref, lnb_ref, o_ref, gated_ref):
    tm = u_ref.shape[1]
    L = SG_CHUNK
    gd = u_ref.shape[2] // SG_GROUPS
    row = lax.broadcasted_iota(jnp.int32, (L, L), 0)
    col = lax.broadcasted_iota(jnp.int32, (L, L), 1)
    causal = col <= row
    for g in range(SG_GROUPS):
        ws = jnp.where(causal, ws_ref[g], 0.0).astype(BF16)
        bias = bst_ref[:, g:g + 1]
        for c in range(tm // L):
            rs = slice(c * L, (c + 1) * L)
            cs = slice(g * gd, (g + 1) * gd)
            mixed = jnp.dot(ws, v_ref[0, rs, cs], preferred_element_type=F32) + bias
            gated_ref[rs, cs] = (u_ref[0, rs, cs].astype(F32) * mixed).astype(BF16)
    y = jnp.dot(gated_ref[...], w_ref[...], preferred_element_type=F32)
    o_ref[0] = _res_ln(x_ref, gate_ref, y, lng_ref, lnb_ref)


def _sg_out(uv, w_s, b_s_t, w_out, x, gate, lng, lnb, *, tm):
    b, s, d = x.shape
    sgd = uv.shape[2] // 2
    full = lambda a: pl.BlockSpec(a.shape, lambda bi, i: (0,) * a.ndim)
    return pl.pallas_call(
        _sg_out_kernel,
        grid=(b, s // tm),
        in_specs=[
            pl.BlockSpec((1, tm, sgd), lambda bi, i: (bi, i, 0)),
            pl.BlockSpec((1, tm, sgd), lambda bi, i: (bi, i, 1)),
            full(w_s), full(b_s_t), full(w_out),
            pl.BlockSpec((1, tm, d), lambda bi, i: (bi, i, 0)),
            pl.BlockSpec((1, 1, d), lambda bi, i: (bi, 0, 0)),
            pl.BlockSpec((1, d), lambda bi, i: (0, 0)),
            pl.BlockSpec((1, d), lambda bi, i: (0, 0)),
        ],
        out_specs=pl.BlockSpec((1, tm, d), lambda bi, i: (bi, i, 0)),
        out_shape=jax.ShapeDtypeStruct((b, s, d), F32),
        scratch_shapes=[pltpu.VMEM((tm, sgd), BF16)],
        compiler_params=_cparams("parallel", "parallel"),
        name="sg_out",
    )(uv, uv, w_s, b_s_t, w_out, x, gate, lng, lnb)


def _swiglu_kernel(x_ref, sc_ref, sh_ref, wg_ref, wu_ref, wd_ref, gate_ref, lng_ref, lnb_ref, o_ref,
                   h_ref, acc_ref):
    f = pl.program_id(2)

    @pl.when(f == 0)
    def _():
        h_ref[...] = _modulate(x_ref, sc_ref, sh_ref).astype(BF16)
        acc_ref[...] = jnp.zeros(acc_ref.shape, F32)

    h = h_ref[...]
    g = jnp.dot(h, wg_ref[...], preferred_element_type=F32)
    u = jnp.dot(h, wu_ref[...], preferred_element_type=F32)
    acc_ref[...] += jnp.dot((_silu(g) * u).astype(BF16), wd_ref[...], preferred_element_type=F32)

    @pl.when(f == pl.num_programs(2) - 1)
    def _():
        o_ref[0] = _res_ln(x_ref, gate_ref, acc_ref[...], lng_ref, lnb_ref)


def _swiglu_res_ln(x, sc, sh, wg, wu, wd, gate, lng, lnb, *, tm, tf):
    b, s, d = x.shape
    ff = wg.shape[1]
    return pl.pallas_call(
        _swiglu_kernel,
        grid=(b, s // tm, ff // tf),
        in_specs=[
            pl.BlockSpec((1, tm, d), lambda bi, i, f: (bi, i, 0)),
            pl.BlockSpec((1, 1, d), lambda bi, i, f: (bi, 0, 0)),
            pl.BlockSpec((1, 1, d), lambda bi, i, f: (bi, 0, 0)),
            pl.BlockSpec((d, tf), lambda bi, i, f: (0, f)),
            pl.BlockSpec((d, tf), lambda bi, i, f: (0, f)),
            pl.BlockSpec((tf, d), lambda bi, i, f: (f, 0)),
            pl.BlockSpec((1, 1, d), lambda bi, i, f: (bi, 0, 0)),
            pl.BlockSpec((1, d), lambda bi, i, f: (0, 0)),
            pl.BlockSpec((1, d), lambda bi, i, f: (0, 0)),
        ],
        out_specs=pl.BlockSpec((1, tm, d), lambda bi, i, f: (bi, i, 0)),
        out_shape=jax.ShapeDtypeStruct((b, s, d), F32),
        scratch_shapes=[pltpu.VMEM((tm, d), BF16), pltpu.VMEM((tm, d), F32)],
        compiler_params=_cparams("parallel", "parallel", "arbitrary"),
        name="swiglu_res_ln",
    )(x, sc, sh, wg, wu, wd, gate, lng, lnb)


def _router_kernel(x_ref, sc_ref, sh_ref, wr_ref, o_ref):
    h = _modulate(x_ref, sc_ref, sh_ref)
    logits = jnp.dot(h, wr_ref[...], preferred_element_type=F32, precision=HIGHEST)
    lane = lax.broadcasted_iota(jnp.int32, logits.shape, 1).astype(F32)
    lg = jnp.where(lane < N_EXPERTS, logits, -jnp.inf)
    m1 = jnp.max(lg, axis=1, keepdims=True)
    i1 = jnp.min(jnp.where(lg == m1, lane, float(LANES)), axis=1, keepdims=True)
    lg2 = jnp.where(lane == i1, -jnp.inf, lg)
    m2 = jnp.max(lg2, axis=1, keepdims=True)
    i2 = jnp.min(jnp.where(lg2 == m2, lane, float(LANES)), axis=1, keepdims=True)
    e2 = jnp.exp(m2 - m1)
    den = 1.0 + e2
    o_ref[0] = jnp.where(lane == i1, 1.0 / den, 0.0) + jnp.where(lane == i2, e2 / den, 0.0)


def _router(x, sc, sh, w_router_pad, *, tm):
    b, s, d = x.shape
    return pl.pallas_call(
        _router_kernel,
        grid=(b, s // tm),
        in_specs=[
            pl.BlockSpec((1, tm, d), lambda bi, i: (bi, i, 0)),
            pl.BlockSpec((1, 1, d), lambda bi, i: (bi, 0, 0)),
            pl.BlockSpec((1, 1, d), lambda bi, i: (bi, 0, 0)),
            pl.BlockSpec((d, LANES), lambda bi, i: (0, 0)),
        ],
        out_specs=pl.BlockSpec((1, tm, LANES), lambda bi, i: (bi, i, 0)),
        out_shape=jax.ShapeDtypeStruct((b, s, LANES), F32),
        compiler_params=_cparams("parallel", "parallel"),
        name="moe_router",
    )(x, sc, sh, w_router_pad)


def _moe_kernel(x_ref, sc_ref, sh_ref, comb_ref, wg_ref, wu_ref, wd_ref, gate_ref, lng_ref, lnb_ref, o_ref,
                h_ref, acc_ref):
    e = pl.program_id(2)
    f = pl.program_id(3)

    @pl.when((e == 0) & (f == 0))
    def _():
        h_ref[...] = _modulate(x_ref, sc_ref, sh_ref).astype(BF16)
        acc_ref[...] = jnp.zeros(acc_ref.shape, F32)

    comb = comb_ref[0]
    lane = lax.broadcasted_iota(jnp.int32, comb.shape, 1)
    ce = jnp.sum(jnp.where(lane == e, comb, 0.0), axis=1, keepdims=True)
    h = h_ref[...]
    g = jnp.dot(h, wg_ref[0], preferred_element_type=F32)
    u = jnp.dot(h, wu_ref[0], preferred_element_type=F32)
    acc_ref[...] += jnp.dot((_silu(g) * u * ce).astype(BF16), wd_ref[0], preferred_element_type=F32)

    @pl.when((e == pl.num_programs(2) - 1) & (f == pl.num_programs(3) - 1))
    def _():
        o_ref[0] = _res_ln(x_ref, gate_ref, acc_ref[...], lng_ref, lnb_ref)


def _moe_res_ln(x, sc, sh, comb, wg, wu, wd, gate, lng, lnb, *, tm, tf):
    b, s, d = x.shape
    ne, _, ff = wg.shape
    return pl.pallas_call(
        _moe_kernel,
        grid=(b, s // tm, ne, ff // tf),
        in_specs=[
            pl.BlockSpec((1, tm, d), lambda bi, i, e, f: (bi, i, 0)),
            pl.BlockSpec((1, 1, d), lambda bi, i, e, f: (bi, 0, 0)),
            pl.BlockSpec((1, 1, d), lambda bi, i, e, f: (bi, 0, 0)),
            pl.BlockSpec((1, tm, LANES), lambda bi, i, e, f: (bi, i, 0)),
            pl.BlockSpec((1, d, tf), lambda bi, i, e, f: (e, 0, f)),
            pl.BlockSpec((1, d, tf), lambda bi, i, e, f: (e, 0, f)),
            pl.BlockSpec((1, tf, d), lambda bi, i, e, f: (e, f, 0)),
            pl.BlockSpec((1, 1, d), lambda bi, i, e, f: (bi, 0, 0)),
            pl.BlockSpec((1, d), lambda bi, i, e, f: (0, 0)),
            pl.BlockSpec((1, d), lambda bi, i, e, f: (0, 0)),
        ],
        out_specs=pl.BlockSpec((1, tm, d), lambda bi, i, e, f: (bi, i, 0)),
        out_shape=jax.ShapeDtypeStruct((b, s, d), F32),
        scratch_shapes=[pltpu.VMEM((tm, d), BF16), pltpu.VMEM((tm, d), F32)],
        compiler_params=_cparams("parallel", "parallel", "arbitrary", "arbitrary"),
        name="moe_res_ln",
    )(x, sc, sh, comb, wg, wu, wd, gate, lng, lnb)


def _tile(s, pref):
    return min(pref, s)


def _ssd_layer(x, sc, sh, gate, lng, lnb, w_in, conv_w, conv_b, dt_bias, a_log, d_skip, norm_w, w_out):
    s = x.shape[1]
    n_heads = dt_bias.shape[0]
    d_inner = n_heads * SSD_HEAD_DIM
    n_zxbc = w_in.shape[1] - n_heads
    pad = LANES - n_heads
    w_zxbc = w_in[:, :n_zxbc].astype(BF16)
    w_dt = jnp.pad(w_in[:, n_zxbc:], ((0, 0), (0, pad))).astype(BF16)
    zxbc, dt_raw = _ssd_in_proj(x, sc, sh, w_zxbc, w_dt, tm=_tile(s, 1024), tn=d_inner)
    yg = _ssd_scan(
        zxbc, dt_raw, conv_w, conv_b[None, :],
        jnp.pad(dt_bias, (0, pad))[None, :], jnp.pad(a_log, (0, pad))[None, :],
        jnp.repeat(d_skip, SSD_HEAD_DIM)[None, :], norm_w[None, :])
    return _mm_res_ln(yg, w_out.astype(BF16), x, gate, lng, lnb, tm=_tile(s, 512), name="ssd_out_proj")


def _mla_layer(x, positions, sc, sh, gate, lng, lnb, w_in, q_norm, kv_norm, w_uq, w_ukv, w_out):
    s = x.shape[1]
    d = x.shape[2]
    nh, hp = MLA_N_HEADS, MLA_HEAD_PAD
    qk = MLA_NOPE + MLA_ROPE
    rope_lo = MLA_NOPE
    w_cq_ckv = w_in[:, :MLA_Q_RANK + MLA_KV_RANK]
    w_kr = jnp.pad(w_in[:, MLA_Q_RANK + MLA_KV_RANK:], ((0, 0), (rope_lo, hp - qk)))
    w_in_pad = jnp.concatenate([w_cq_ckv, w_kr], axis=1).astype(BF16)
    cin = _mod_matmul(x, sc, sh, w_in_pad, tm=_tile(s, 1024), tn=MLA_IN_PAD, out_dtype=F32, name="mla_in_proj")
    w_uq_pad = jnp.pad(w_uq.reshape(MLA_Q_RANK, nh, qk), ((0, 0), (0, 0), (0, hp - qk)))
    w_uq_pad = w_uq_pad.reshape(MLA_Q_RANK, nh * hp).astype(BF16)
    w_ukv3 = w_ukv.reshape(MLA_KV_RANK, nh, MLA_NOPE + MLA_V)
    w_uk_pad = jnp.pad(w_ukv3[:, :, :MLA_NOPE], ((0, 0), (0, 0), (0, hp - MLA_NOPE)))
    w_uk_pad = w_uk_pad.reshape(MLA_KV_RANK, nh * hp).astype(BF16)
    w_uv = w_ukv3[:, :, MLA_NOPE:].reshape(MLA_KV_RANK, nh * MLA_V).astype(BF16)
    half = MLA_ROPE // 2
    freqs = ROPE_THETA ** (-jnp.arange(half, dtype=F32) / half)
    zeros = lambda n: jnp.zeros((n,), F32)
    freq_row = jnp.concatenate([zeros(rope_lo), freqs, freqs, zeros(hp - qk)])[None, :]
    sign_row = jnp.concatenate([zeros(rope_lo), -jnp.ones((half,), F32), jnp.ones((half,), F32),
                                zeros(hp - qk)])[None, :]
    q, k, v = _mla_proj(cin, positions[:, :, None], freq_row, sign_row, q_norm[None, :], kv_norm[None, :],
                        w_uq_pad, w_uk_pad, w_uv, tm=_tile(s, 512))
    attn = _attention(q, k, v, tq=_tile(s, 512))
    return _mm_res_ln(attn, w_out.astype(BF16), x, gate, lng, lnb, tm=_tile(s, 512), name="mla_out_proj")


def _sg_layer(x, sc, sh, gate, lng, lnb, w_in, b_in, ln_g, ln_b, w_s, b_s, w_out):
    s = x.shape[1]
    uv = _sg_in_proj(x, sc, sh, w_in.astype(BF16), b_in[None, :], ln_g[None, :], ln_b[None, :], tm=_tile(s, 512))
    return _sg_out(uv, w_s, b_s.T, w_out.astype(BF16), x, gate, lng, lnb, tm=_tile(s, 512))


def _dense_ffn(x, sc, sh, gate, lng, lnb, w_gate, w_up, w_down):
    s = x.shape[1]
    ff = w_gate.shape[1]
    return _swiglu_res_ln(x, sc, sh, w_gate.astype(BF16), w_up.astype(BF16), w_down.astype(BF16),
                          gate, lng, lnb, tm=_tile(s, 512), tf=ff // 2)


def _moe_ffn(x, sc, sh, gate, lng, lnb, w_router, w_gate, w_up, w_down):
    s = x.shape[1]
    ff = w_gate.shape[2]
    w_router_pad = jnp.pad(w_router, ((0, 0), (0, LANES - w_router.shape[1])))
    comb = _router(x, sc, sh, w_router_pad, tm=_tile(s, 512))
    return _moe_res_ln(x, sc, sh, comb, w_gate.astype(BF16), w_up.astype(BF16), w_down.astype(BF16),
                       gate, lng, lnb, tm=_tile(s, 512), tf=ff // 2)


def kernel(x, c, positions, ada_w, ada_b, ln_g, ln_b, ssd_w_in, ssd_conv_w, ssd_conv_b, ssd_dt_bias, ssd_a_log, ssd_d_skip, ssd_norm_w, ssd_w_out, mla_w_in, mla_q_norm, mla_kv_norm, mla_w_uq, mla_w_ukv, mla_w_out, sg_w_in, sg_b_in, sg_ln_g, sg_ln_b, sg_w_s, sg_b_s, sg_w_out, ffn_w_gate, ffn_w_up, ffn_w_down, moe_w_router, moe_w_gate, moe_w_up, moe_w_down):
    batch = x.shape[0]
    depth = ada_w.shape[0]
    c_pad = jnp.pad(c, ((0, SUBLANES - batch), (0, 0)))
    mod = _ada_mod(c_pad, ada_w, ada_b)[:, :, :batch]
    for i in range(depth):
        sh_m, sc_m, g_m, sh_f, sc_f, g_f = [mod[i, t][:, None, :] for t in range(6)]
        lng_m, lnb_m = ln_g[i, 0][None, :], ln_b[i, 0][None, :]
        lng_f, lnb_f = ln_g[i, 1][None, :], ln_b[i, 1][None, :]
        kind, j = i % 3, i // 3
        if kind == 0:
            x = _ssd_layer(x, sc_m, sh_m, g_m, lng_m, lnb_m, ssd_w_in[j], ssd_conv_w[j], ssd_conv_b[j],
                           ssd_dt_bias[j], ssd_a_log[j], ssd_d_skip[j], ssd_norm_w[j], ssd_w_out[j])
        elif kind == 1:
            x = _mla_layer(x, positions, sc_m, sh_m, g_m, lng_m, lnb_m, mla_w_in[j], mla_q_norm[j],
                           mla_kv_norm[j], mla_w_uq[j], mla_w_ukv[j], mla_w_out[j])
        else:
            x = _sg_layer(x, sc_m, sh_m, g_m, lng_m, lnb_m, sg_w_in[j], sg_b_in[j], sg_ln_g[j], sg_ln_b[j],
                          sg_w_s[j], sg_b_s[j], sg_w_out[j])
        k = i // 2
        if i % 2 == 0:
            x = _dense_ffn(x, sc_f, sh_f, g_f, lng_f, lnb_f, ffn_w_gate[k], ffn_w_up[k], ffn_w_down[k])
        else:
            x = _moe_ffn(x, sc_f, sh_f, g_f, lng_f, lnb_f, moe_w_router[k], moe_w_gate[k], moe_w_up[k],
                         moe_w_down[k])
    return x
```

```python
import functools
import math

import jax
import jax.numpy as jnp
from jax import lax
from jax.experimental import pallas as pl
from jax.experimental.pallas import tpu as pltpu

F32 = jnp.float32
BF16 = jnp.bfloat16
HIGHEST = lax.Precision.HIGHEST

DEPTH = 4
ALPHA = (2.0 * DEPTH) ** 0.25
LN_EPS = 1e-5
RMS_EPS = 1e-6
ROPE_THETA = 10000.0

LANES = 128
SUBLANES = 8
VMEM_LIMIT = 56 * 1024 * 1024

SSD_HEAD_DIM = 64
SSD_N_GROUPS = 8
SSD_HPG = 4
SSD_D_STATE = 128
SSD_CONV = 4
SSD_CHUNK = 128
SSD_GROUP_W = SSD_HPG * SSD_HEAD_DIM

MLA_N_HEADS = 16
MLA_NOPE = 64
MLA_ROPE = 32
MLA_V = 64
MLA_Q_RANK = 512
MLA_KV_RANK = 256
MLA_HEAD_PAD = 128
MLA_IN_PAD = MLA_Q_RANK + MLA_KV_RANK + MLA_HEAD_PAD
QK_SCALE = (MLA_NOPE + MLA_ROPE) ** -0.5
LOG2E = 1.4426950408889634
NEG_BIG = -1e30

SG_GROUPS = 8
SG_CHUNK = 128

N_EXPERTS = 8
MOE_BLOCK = 1024
MOE_ROWS = 128
MOE_GROUP = 4


def _cparams(*sem):
    return pltpu.CompilerParams(dimension_semantics=sem, vmem_limit_bytes=VMEM_LIMIT)


def _layer_norm(r):
    mu = jnp.mean(r, axis=-1, keepdims=True)
    d = r - mu
    var = jnp.mean(d * d, axis=-1, keepdims=True)
    return d * lax.rsqrt(var + LN_EPS)


def _silu(v):
    return v * jax.nn.sigmoid(v)


def _modulate(x_ref, sc_ref, sh_ref):
    return x_ref[0] * (1.0 + sc_ref[0]) + sh_ref[0]


def _res_ln(x_ref, gate_ref, y, lng_ref, lnb_ref):
    r = ALPHA * x_ref[0] + (1.0 + gate_ref[0]) * y
    return _layer_norm(r) * lng_ref[...] + lnb_ref[...]


def _ada_kernel(c_ref, w_ref, b_ref, o_ref):
    cond = _silu(c_ref[...])
    o_ref[0, 0] = jnp.dot(cond, w_ref[0], preferred_element_type=F32, precision=HIGHEST) + b_ref[0, 0]


def _ada_mod(c_pad, ada_w, ada_b):
    depth, d, _ = ada_w.shape
    rows = c_pad.shape[0]
    return pl.pallas_call(
        _ada_kernel,
        grid=(depth, 6),
        in_specs=[
            pl.BlockSpec((rows, d), lambda i, j: (0, 0)),
            pl.BlockSpec((1, d, d), lambda i, j: (i, 0, j)),
            pl.BlockSpec((1, 1, 1, d), lambda i, j: (i, j, 0, 0)),
        ],
        out_specs=pl.BlockSpec((1, 1, rows, d), lambda i, j: (i, j, 0, 0)),
        out_shape=jax.ShapeDtypeStruct((depth, 6, rows, d), F32),
        compiler_params=_cparams("arbitrary", "arbitrary"),
        name="ada_mod",
    )(c_pad, ada_w, ada_b.reshape(depth, 6, 1, d))


def _modmm_kernel(x_ref, sc_ref, sh_ref, w_ref, o_ref, h_ref):
    @pl.when(pl.program_id(2) == 0)
    def _():
        h_ref[...] = _modulate(x_ref, sc_ref, sh_ref).astype(BF16)

    o_ref[0] = jnp.dot(h_ref[...], w_ref[...], preferred_element_type=F32).astype(o_ref.dtype)


def _mod_matmul(x, sc, sh, w, *, tm, tn, out_dtype, name):
    b, s, d = x.shape
    n = w.shape[1]
    return pl.pallas_call(
        _modmm_kernel,
        grid=(b, s // tm, n // tn),
        in_specs=[
            pl.BlockSpec((1, tm, d), lambda bi, i, j: (bi, i, 0)),
            pl.BlockSpec((1, 1, d), lambda bi, i, j: (bi, 0, 0)),
            pl.BlockSpec((1, 1, d), lambda bi, i, j: (bi, 0, 0)),
            pl.BlockSpec((d, tn), lambda bi, i, j: (0, j)),
        ],
        out_specs=pl.BlockSpec((1, tm, tn), lambda bi, i, j: (bi, i, j)),
        out_shape=jax.ShapeDtypeStruct((b, s, n), out_dtype),
        scratch_shapes=[pltpu.VMEM((tm, d), BF16)],
        compiler_params=_cparams("parallel", "parallel", "arbitrary"),
        name=name,
    )(x, sc, sh, w)


def _ssd_in_kernel(x_ref, sc_ref, sh_ref, w_ref, wdt_ref, o_ref, dt_ref, h_ref):
    @pl.when(pl.program_id(2) == 0)
    def _():
        h_ref[...] = _modulate(x_ref, sc_ref, sh_ref).astype(BF16)
        dt_ref[0] = jnp.dot(h_ref[...], wdt_ref[...], preferred_element_type=F32)

    o_ref[0] = jnp.dot(h_ref[...], w_ref[...], preferred_element_type=F32).astype(o_ref.dtype)


def _ssd_in_proj(x, sc, sh, w_zxbc, w_dt, *, tm, tn):
    b, s, d = x.shape
    n = w_zxbc.shape[1]
    return pl.pallas_call(
        _ssd_in_kernel,
        grid=(b, s // tm, n // tn),
        in_specs=[
            pl.BlockSpec((1, tm, d), lambda bi, i, j: (bi, i, 0)),
            pl.BlockSpec((1, 1, d), lambda bi, i, j: (bi, 0, 0)),
            pl.BlockSpec((1, 1, d), lambda bi, i, j: (bi, 0, 0)),
            pl.BlockSpec((d, tn), lambda bi, i, j: (0, j)),
            pl.BlockSpec((d, LANES), lambda bi, i, j: (0, 0)),
        ],
        out_specs=[
            pl.BlockSpec((1, tm, tn), lambda bi, i, j: (bi, i, j)),
            pl.BlockSpec((1, tm, LANES), lambda bi, i, j: (bi, i, 0)),
        ],
        out_shape=[
            jax.ShapeDtypeStruct((b, s, n), BF16),
            jax.ShapeDtypeStruct((b, s, LANES), F32),
        ],
        scratch_shapes=[pltpu.VMEM((tm, d), BF16)],
        compiler_params=_cparams("parallel", "parallel", "arbitrary"),
        name="ssd_in_proj",
    )(x, sc, sh, w_zxbc, w_dt)


def _ssd_kernel(z_ref, x_ref, bc_ref, dt_ref, cwx_ref, cwbc_ref, cbx_ref, cbbc_ref, dtb_ref, alog_ref,
                dskip_ref, nw_ref, o_ref, extx_ref, extbc_ref, xs_ref, b_ref, c_ref, state_ref):
    L = SSD_CHUNK
    gw = SSD_GROUP_W
    n_st = SSD_D_STATE
    tail = SUBLANES
    ci = pl.program_id(1)

    @pl.when(ci == 0)
    def _():
        extx_ref[0:tail] = jnp.zeros((tail, extx_ref.shape[1]), F32)
        extbc_ref[0:tail] = jnp.zeros((tail, extbc_ref.shape[1]), F32)
        state_ref[...] = jnp.zeros(state_ref.shape, F32)

    @pl.when(ci != 0)
    def _():
        extx_ref[0:tail] = extx_ref[L:L + tail]
        extbc_ref[0:tail] = extbc_ref[L:L + tail]

    extx_ref[tail:tail + L] = x_ref[0].astype(F32)
    extbc_ref[tail:tail + L] = bc_ref[0].astype(F32)

    def conv(ext_ref, cw_ref, cb_ref):
        acc = cb_ref[...]
        for k in range(SSD_CONV):
            off = tail - (SSD_CONV - 1) + k
            acc = acc + ext_ref[off:off + L, :] * cw_ref[k:k + 1, :]
        return _silu(acc)

    xs_ref[...] = conv(extx_ref, cwx_ref, cbx_ref)
    bc = conv(extbc_ref, cwbc_ref, cbbc_ref)
    half = bc.shape[1] // 2
    b_ref[...] = bc[:, :half]
    c_ref[...] = bc[:, half:].astype(BF16)

    dtv = dt_ref[0] + dtb_ref[...]
    dt = jnp.maximum(dtv, 0.0) + jnp.log1p(jnp.exp(-jnp.abs(dtv)))
    a_neg = -jnp.exp(alog_ref[...])
    row = lax.broadcasted_iota(jnp.int32, (L, L), 0)
    col = lax.broadcasted_iota(jnp.int32, (L, L), 1)
    causal = col <= row
    tri = jnp.where(causal, 1.0, 0.0).astype(F32)
    a_cum = jnp.dot(tri, dt * a_neg, preferred_element_type=F32, precision=HIGHEST)
    a_cum_t = a_cum.T
    ea = jnp.exp(a_cum)
    dte = jnp.exp(a_cum[L - 1:L, :] - a_cum)

    lane = lax.broadcasted_iota(jnp.int32, (L, LANES), 1)
    head_of_lane = lax.broadcasted_iota(jnp.int32, (L, gw), 1) // SSD_HEAD_DIM

    def expand(v, g):
        def pair(h0, h1):
            return jnp.where(lane < SSD_HEAD_DIM,
                             jnp.broadcast_to(v[:, h0:h0 + 1], (L, LANES)),
                             jnp.broadcast_to(v[:, h1:h1 + 1], (L, LANES)))
        h = SSD_HPG * g
        return jnp.concatenate([pair(h, h + 1), pair(h + 2, h + 3)], axis=1)

    for g in range(SSD_N_GROUPS):
        sl = slice(g * gw, (g + 1) * gw)
        sn = slice(g * n_st, (g + 1) * n_st)
        c_g = c_ref[:, sn]
        b_g32 = b_ref[:, sn]
        b_g = b_g32.astype(BF16)
        b_gt = b_g32.T.astype(BF16)
        cb = lax.dot_general(c_g, b_g, (((1,), (1,)), ((), ())), preferred_element_type=F32)
        st = state_ref[g]
        y_off = jnp.dot(c_g, st.astype(BF16), preferred_element_type=F32)
        xs_g = xs_ref[:, sl]
        dt_g = expand(dt, g)
        dte_g = expand(dte, g)
        ea_g = expand(ea, g)
        xdt = xs_g * dt_g
        y_diag = jnp.zeros((L, gw), F32)
        for r in range(SSD_HPG):
            h = SSD_HPG * g + r
            seg = a_cum[:, h:h + 1] - a_cum_t[h:h + 1, :]
            decay = jnp.exp(jnp.where(causal, seg, NEG_BIG))
            m = (cb * decay).astype(BF16)
            x_r = jnp.where(head_of_lane == r, xdt, 0.0).astype(BF16)
            y_diag = y_diag + jnp.dot(m, x_r, preferred_element_type=F32)
        y = y_diag + y_off * ea_g + xs_g * dskip_ref[:, sl]
        z_g = z_ref[0, :, sl].astype(F32)
        t = y * _silu(z_g)
        ms = jnp.mean(t * t, axis=-1, keepdims=True)
        o_ref[0, :, sl] = (t * lax.rsqrt(ms + RMS_EPS) * nw_ref[:, sl]).astype(o_ref.dtype)
        new_st = jnp.dot(b_gt, (xdt * dte_g).astype(BF16), preferred_element_type=F32)
        state_ref[g] = st * ea_g[L - 1:L, :] + new_st


def _ssd_scan(zxbc, dt_raw, conv_w, conv_b, dt_bias, a_log, d_skip, norm_w):
    b, s, _ = zxbc.shape
    L = SSD_CHUNK
    di = SSD_N_GROUPS * SSD_GROUP_W
    blk = lambda j: pl.BlockSpec((1, L, di), lambda bi, ci, j=j: (bi, ci, j))
    vec = lambda w: pl.BlockSpec((1, w), lambda bi, ci: (0, 0))
    cw = lambda j: pl.BlockSpec((SSD_CONV, di), lambda bi, ci, j=j: (0, j))
    cb = lambda j: pl.BlockSpec((1, di), lambda bi, ci, j=j: (0, j))
    return pl.pallas_call(
        _ssd_kernel,
        grid=(b, s // L),
        in_specs=[
            blk(0), blk(1), blk(2),
            pl.BlockSpec((1, L, LANES), lambda bi, ci: (bi, ci, 0)),
            cw(0), cw(1), cb(0), cb(1),
            vec(LANES), vec(LANES), vec(di), vec(di),
        ],
        out_specs=pl.BlockSpec((1, L, di), lambda bi, ci: (bi, ci, 0)),
        out_shape=jax.ShapeDtypeStruct((b, s, di), BF16),
        scratch_shapes=[
            pltpu.VMEM((L + SUBLANES, di), F32),
            pltpu.VMEM((L + SUBLANES, di), F32),
            pltpu.VMEM((L, di), F32),
            pltpu.VMEM((L, di // 2), F32),
            pltpu.VMEM((L, di // 2), BF16),
            pltpu.VMEM((SSD_N_GROUPS, SSD_D_STATE, SSD_GROUP_W), F32),
        ],
        compiler_params=_cparams("arbitrary", "arbitrary"),
        name="ssd_scan",
    )(zxbc, zxbc, zxbc, dt_raw, conv_w, conv_w, conv_b, conv_b, dt_bias, a_log, d_skip, norm_w)


def _mm_res_ln_kernel(a_ref, w_ref, x_ref, gate_ref, lng_ref, lnb_ref, o_ref):
    y = jnp.dot(a_ref[0], w_ref[...], preferred_element_type=F32)
    o_ref[0] = _res_ln(x_ref, gate_ref, y, lng_ref, lnb_ref)


def _mm_res_ln(a, w, x, gate, lng, lnb, *, tm, name):
    b, s, d = x.shape
    k = a.shape[2]
    return pl.pallas_call(
        _mm_res_ln_kernel,
        grid=(b, s // tm),
        in_specs=[
            pl.BlockSpec((1, tm, k), lambda bi, i: (bi, i, 0)),
            pl.BlockSpec((k, d), lambda bi, i: (0, 0)),
            pl.BlockSpec((1, tm, d), lambda bi, i: (bi, i, 0)),
            pl.BlockSpec((1, 1, d), lambda bi, i: (bi, 0, 0)),
            pl.BlockSpec((1, d), lambda bi, i: (0, 0)),
            pl.BlockSpec((1, d), lambda bi, i: (0, 0)),
        ],
        out_specs=pl.BlockSpec((1, tm, d), lambda bi, i: (bi, i, 0)),
        out_shape=jax.ShapeDtypeStruct((b, s, d), F32),
        compiler_params=_cparams("parallel", "parallel"),
        name=name,
    )(a, w, x, gate, lng, lnb)


def _mla_proj_kernel(cin_ref, pos_ref, freq_ref, sign_ref, qn_ref, kvn_ref, wuq_ref, wuk_ref, wuv_ref,
                     q_ref, k_ref, v_ref):
    cin = cin_ref[0]
    tm = cin.shape[0]
    hp = MLA_HEAD_PAD

    def rms(v, w_ref):
        return v * lax.rsqrt(jnp.mean(v * v, axis=-1, keepdims=True) + RMS_EPS) * w_ref[...]

    cq = rms(cin[:, :MLA_Q_RANK], qn_ref).astype(BF16)
    ckv = rms(cin[:, MLA_Q_RANK:MLA_Q_RANK + MLA_KV_RANK], kvn_ref).astype(BF16)
    k_rope = cin[:, MLA_Q_RANK + MLA_KV_RANK:]
    q = jnp.dot(cq, wuq_ref[...], preferred_element_type=F32)
    k = jnp.dot(ckv, wuk_ref[...], preferred_element_type=F32)
    ones_col = jnp.where(lax.broadcasted_iota(jnp.int32, (1, wuv_ref.shape[1]), 1) % hp == MLA_V, 1.0, 0.0)
    v_ref[0] = (jnp.dot(ckv, wuv_ref[...], preferred_element_type=F32) + ones_col).astype(v_ref.dtype)

    ang = pos_ref[0].astype(F32) * freq_ref[...]
    cos = jnp.cos(ang)
    sin = jnp.sin(ang) * sign_ref[...]
    lane = lax.broadcasted_iota(jnp.int32, (tm, hp), 1)
    first_half = lane < MLA_NOPE + MLA_ROPE // 2

    def rope(xh):
        swapped = jnp.where(first_half,
                            pltpu.roll(xh, hp - MLA_ROPE // 2, 1),
                            pltpu.roll(xh, MLA_ROPE // 2, 1))
        return xh * cos + swapped * sin

    kr = rope(k_rope)
    for h in range(MLA_N_HEADS):
        sl = slice(h * hp, (h + 1) * hp)
        q_ref[0, :, sl] = (rope(q[:, sl]) * (QK_SCALE * LOG2E)).astype(q_ref.dtype)
        k_ref[0, :, sl] = (k[:, sl] + kr).astype(k_ref.dtype)


def _mla_proj(cin, pos, freq, sign, q_norm, kv_norm, w_uq, w_uk, w_uv, *, tm):
    b, s, n_in = cin.shape
    nq = w_uq.shape[1]
    nv = w_uv.shape[1]
    full = lambda a: pl.BlockSpec(a.shape, lambda bi, i: (0,) * a.ndim)
    return pl.pallas_call(
        _mla_proj_kernel,
        grid=(b, s // tm),
        in_specs=[
            pl.BlockSpec((1, tm, n_in), lambda bi, i: (bi, i, 0)),
            pl.BlockSpec((1, tm, 1), lambda bi, i: (bi, i, 0)),
            full(freq), full(sign), full(q_norm), full(kv_norm), full(w_uq), full(w_uk), full(w_uv),
        ],
        out_specs=[
            pl.BlockSpec((1, tm, nq), lambda bi, i: (bi, i, 0)),
            pl.BlockSpec((1, tm, nq), lambda bi, i: (bi, i, 0)),
            pl.BlockSpec((1, tm, nv), lambda bi, i: (bi, i, 0)),
        ],
        out_shape=[
            jax.ShapeDtypeStruct((b, s, nq), BF16),
            jax.ShapeDtypeStruct((b, s, nq), BF16),
            jax.ShapeDtypeStruct((b, s, nv), BF16),
        ],
        compiler_params=_cparams("parallel", "parallel"),
        name="mla_proj",
    )(cin, pos, freq, sign, q_norm, kv_norm, w_uq, w_uk, w_uv)


def _attn_kernel(q_ref, k_ref, v_ref, o_ref, acc_ref, s0_ref, s1_ref, *, tq, n_kb):
    qi = pl.program_id(2)
    hp = MLA_HEAD_PAD
    key_minus_query = (lax.broadcasted_iota(jnp.int32, (tq, tq), 1)
                       - lax.broadcasted_iota(jnp.int32, (tq, tq), 0))

    def scores(kb, s_ref):
        start = pl.multiple_of(jnp.minimum(kb, n_kb - 1) * tq, tq)
        keep = key_minus_query <= (qi - kb) * tq
        for hh in range(2):
            q = q_ref[0, :, hh * hp:(hh + 1) * hp]
            k = k_ref[0, pl.ds(start, tq), hh * hp:(hh + 1) * hp]
            s = lax.dot_general(q, k, (((1,), (1,)), ((), ())), preferred_element_type=F32)
            s_ref[hh] = jnp.where(keep, s, NEG_BIG)

    def softmax_pv(kb, s_ref, m_prev):
        start = pl.multiple_of(jnp.minimum(kb, n_kb - 1) * tq, tq)
        m_out = []
        for hh in range(2):
            v = v_ref[0, pl.ds(start, tq), hh * hp:(hh + 1) * hp]
            s = s_ref[hh]
            m_new = jnp.maximum(m_prev[hh], jnp.max(s, axis=1, keepdims=True))
            alpha = jnp.exp2(m_prev[hh] - m_new)
            p = jnp.exp2(s - m_new).astype(BF16)
            acc_ref[hh] = acc_ref[hh] * alpha + jnp.dot(p, v, preferred_element_type=F32)
            m_out.append(m_new)
        return tuple(m_out)

    acc_ref[...] = jnp.zeros(acc_ref.shape, F32)
    m_init = jnp.full((tq, 1), NEG_BIG, F32)
    scores(0, s0_ref)

    def two_blocks(t, m):
        m = softmax_pv(2 * t, s0_ref, m)
        scores(2 * t + 1, s1_ref)
        m = softmax_pv(2 * t + 1, s1_ref, m)
        scores(2 * t + 2, s0_ref)
        return m

    lax.fori_loop(0, (qi + 2) // 2, two_blocks, (m_init, m_init))
    out_a = acc_ref[0] / acc_ref[0][:, MLA_V:MLA_V + 1]
    out_b = acc_ref[1] / acc_ref[1][:, MLA_V:MLA_V + 1]
    lane = lax.broadcasted_iota(jnp.int32, (tq, hp), 1)
    o_ref[0] = jnp.where(lane < MLA_V, out_a, pltpu.roll(out_b, MLA_V, 1)).astype(o_ref.dtype)


def _attention(q, k, v, *, tq):
    b, s, _ = q.shape
    pairs = MLA_N_HEADS // 2
    return pl.pallas_call(
        functools.partial(_attn_kernel, tq=tq, n_kb=s // tq),
        grid=(b, pairs, s // tq),
        in_specs=[
            pl.BlockSpec((1, tq, 2 * MLA_HEAD_PAD), lambda bi, j, i: (bi, i, j)),
            pl.BlockSpec((1, s, 2 * MLA_HEAD_PAD), lambda bi, j, i: (bi, 0, j)),
            pl.BlockSpec((1, s, 2 * MLA_HEAD_PAD), lambda bi, j, i: (bi, 0, j)),
        ],
        out_specs=pl.BlockSpec((1, tq, 2 * MLA_V), lambda bi, j, i: (bi, i, j)),
        out_shape=jax.ShapeDtypeStruct((b, s, MLA_N_HEADS * MLA_V), BF16),
        scratch_shapes=[pltpu.VMEM((2, tq, MLA_HEAD_PAD), F32), pltpu.VMEM((2, tq, tq), F32),
                        pltpu.VMEM((2, tq, tq), F32)],
        compiler_params=_cparams("parallel", "parallel", "arbitrary"),
        name="mla_attention",
    )(q, k, v)


def _sg_in_kernel(x_ref, sc_ref, sh_ref, w_ref, b_ref, lng_ref, lnb_ref, o_ref, h_ref):
    j = pl.program_id(2)

    @pl.when(j == 0)
    def _():
        h_ref[...] = _modulate(x_ref, sc_ref, sh_ref).astype(BF16)

    y = jax.nn.gelu(jnp.dot(h_ref[...], w_ref[...], preferred_element_type=F32) + b_ref[...])

    @pl.when(j == 0)
    def _():
        o_ref[0] = y.astype(o_ref.dtype)

    @pl.when(j == 1)
    def _():
        o_ref[0] = (_layer_norm(y) * lng_ref[...] + lnb_ref[...]).astype(o_ref.dtype)


def _sg_in_proj(x, sc, sh, w, bias, lng, lnb, *, tm):
    b, s, d = x.shape
    n = w.shape[1]
    tn = n // 2
    return pl.pallas_call(
        _sg_in_kernel,
        grid=(b, s // tm, 2),
        in_specs=[
            pl.BlockSpec((1, tm, d), lambda bi, i, j: (bi, i, 0)),
            pl.BlockSpec((1, 1, d), lambda bi, i, j: (bi, 0, 0)),
            pl.BlockSpec((1, 1, d), lambda bi, i, j: (bi, 0, 0)),
            pl.BlockSpec((d, tn), lambda bi, i, j: (0, j)),
            pl.BlockSpec((1, tn), lambda bi, i, j: (0, j)),
            pl.BlockSpec((1, tn), lambda bi, i, j: (0, 0)),
            pl.BlockSpec((1, tn), lambda bi, i, j: (0, 0)),
        ],
        out_specs=pl.BlockSpec((1, tm, tn), lambda bi, i, j: (bi, i, j)),
        out_shape=jax.ShapeDtypeStruct((b, s, n), BF16),
        scratch_shapes=[pltpu.VMEM((tm, d), BF16)],
        compiler_params=_cparams("parallel", "parallel", "arbitrary"),
        name="sg_in_proj",
    )(x, sc, sh, w, bias, lng, lnb)


def _sg_out_kernel(u_ref, v_ref, ws_ref, bst_ref, w_ref, x_ref, gate_ref, lng_ref, lnb_ref, o_ref, gated_ref):
    tm = u_ref.shape[1]
    L = SG_CHUNK
    gd = u_ref.shape[2] // SG_GROUPS
    row = lax.broadcasted_iota(jnp.int32, (L, L), 0)
    col = lax.broadcasted_iota(jnp.int32, (L, L), 1)
    causal = col <= row
    for g in range(SG_GROUPS):
        ws = jnp.where(causal, ws_ref[g], 0.0).astype(BF16)
        bias = bst_ref[:, g:g + 1]
        for c in range(tm // L):
            rs = slice(c * L, (c + 1) * L)
            cs = slice(g * gd, (g + 1) * gd)
            mixed = jnp.dot(ws, v_ref[0, rs, cs], preferred_element_type=F32) + bias
            gated_ref[rs, cs] = (u_ref[0, rs, cs].astype(F32) * mixed).astype(BF16)
    y = jnp.dot(gated_ref[...], w_ref[...], preferred_element_type=F32)
    o_ref[0] = _res_ln(x_ref, gate_ref, y, lng_ref, lnb_ref)


def _sg_out(uv, w_s, b_s_t, w_out, x, gate, lng, lnb, *, tm):
    b, s, d = x.shape
    sgd = uv.shape[2] // 2
    full = lambda a: pl.BlockSpec(a.shape, lambda bi, i: (0,) * a.ndim)
    return pl.pallas_call(
        _sg_out_kernel,
        grid=(b, s // tm),
        in_specs=[
            pl.BlockSpec((1, tm, sgd), lambda bi, i: (bi, i, 0)),
            pl.BlockSpec((1, tm, sgd), lambda bi, i: (bi, i, 1)),
            full(w_s), full(b_s_t), full(w_out),
            pl.BlockSpec((1, tm, d), lambda bi, i: (bi, i, 0)),
            pl.BlockSpec((1, 1, d), lambda bi, i: (bi, 0, 0)),
            pl.BlockSpec((1, d), lambda bi, i: (0, 0)),
            pl.BlockSpec((1, d), lambda bi, i: (0, 0)),
        ],
        out_specs=pl.BlockSpec((1, tm, d), lambda bi, i: (bi, i, 0)),
        out_shape=jax.ShapeDtypeStruct((b, s, d), F32),
        scratch_shapes=[pltpu.VMEM((tm, sgd), BF16)],
        compiler_params=_cparams("parallel", "parallel"),
        name="sg_out",
    )(uv, uv, w_s, b_s_t, w_out, x, gate, lng, lnb)


def _swiglu_kernel(x_ref, sc_ref, sh_ref, wg_ref, wu_ref, wd_ref, gate_ref, lng_ref, lnb_ref, o_ref,
                   h_ref, acc_ref):
    f = pl.program_id(2)

    @pl.when(f == 0)
    def _():
        h_ref[...] = _modulate(x_ref, sc_ref, sh_ref).astype(BF16)
        acc_ref[...] = jnp.zeros(acc_ref.shape, F32)

    h = h_ref[...]
    g = jnp.dot(h, wg_ref[...], preferred_element_type=F32)
    u = jnp.dot(h, wu_ref[...], preferred_element_type=F32)
    acc_ref[...] += jnp.dot((_silu(g) * u).astype(BF16), wd_ref[...], preferred_element_type=F32)

    @pl.when(f == pl.num_programs(2) - 1)
    def _():
        o_ref[0] = _res_ln(x_ref, gate_ref, acc_ref[...], lng_ref, lnb_ref)


def _swiglu_res_ln(x, sc, sh, wg, wu, wd, gate, lng, lnb, *, tm, tf):
    b, s, d = x.shape
    ff = wg.shape[1]
    return pl.pallas_call(
        _swiglu_kernel,
        grid=(b, s // tm, ff // tf),
        in_specs=[
            pl.BlockSpec((1, tm, d), lambda bi, i, f: (bi, i, 0)),
            pl.BlockSpec((1, 1, d), lambda bi, i, f: (bi, 0, 0)),
            pl.BlockSpec((1, 1, d), lambda bi, i, f: (bi, 0, 0)),
            pl.BlockSpec((d, tf), lambda bi, i, f: (0, f)),
            pl.BlockSpec((d, tf), lambda bi, i, f: (0, f)),
            pl.BlockSpec((tf, d), lambda bi, i, f: (f, 0)),
            pl.BlockSpec((1, 1, d), lambda bi, i, f: (bi, 0, 0)),
            pl.BlockSpec((1, d), lambda bi, i, f: (0, 0)),
            pl.BlockSpec((1, d), lambda bi, i, f: (0, 0)),
        ],
        out_specs=pl.BlockSpec((1, tm, d), lambda bi, i, f: (bi, i, 0)),
        out_shape=jax.ShapeDtypeStruct((b, s, d), F32),
        scratch_shapes=[pltpu.VMEM((tm, d), BF16), pltpu.VMEM((tm, d), F32)],
        compiler_params=_cparams("parallel", "parallel", "arbitrary"),
        name="swiglu_res_ln",
    )(x, sc, sh, wg, wu, wd, gate, lng, lnb)


def _router_kernel(x_ref, sc_ref, sh_ref, wr_ref, h_ref, comb_ref, pos_ref, post_ref, cnt_ref):
    h = _modulate(x_ref, sc_ref, sh_ref)
    h_ref[0] = h.astype(h_ref.dtype)
    logits = jnp.dot(h, wr_ref[...], preferred_element_type=F32, precision=HIGHEST)
    tm = logits.shape[0]
    lane = lax.broadcasted_iota(jnp.int32, logits.shape, 1).astype(F32)
    lg = jnp.where(lane < N_EXPERTS, logits, -jnp.inf)
    m1 = jnp.max(lg, axis=1, keepdims=True)
    i1 = jnp.min(jnp.where(lg == m1, lane, float(LANES)), axis=1, keepdims=True)
    lg2 = jnp.where(lane == i1, -jnp.inf, lg)
    m2 = jnp.max(lg2, axis=1, keepdims=True)
    i2 = jnp.min(jnp.where(lg2 == m2, lane, float(LANES)), axis=1, keepdims=True)
    e2 = jnp.exp(m2 - m1)
    den = 1.0 + e2
    comb_ref[0] = jnp.where(lane == i1, 1.0 / den, 0.0) + jnp.where(lane == i2, e2 / den, 0.0)
    sel = (lane == i1) | (lane == i2)
    earlier = (lax.broadcasted_iota(jnp.int32, (tm, tm), 1) < lax.broadcasted_iota(jnp.int32, (tm, tm), 0))
    sel_f = jnp.where(sel, 1.0, 0.0)
    rank = jnp.dot(jnp.where(earlier, 1.0, 0.0).astype(BF16), sel_f.astype(BF16), preferred_element_type=F32)
    pos = jnp.where(sel, rank, -1.0)
    pos_ref[0] = pos
    post_ref[0] = pos.T[0:SUBLANES, :]
    cnt_ref[0] = jnp.sum(sel_f, axis=0, keepdims=True)


def _router(x, sc, sh, w_router_pad, *, tm):
    b, s, d = x.shape
    nbs = s // tm
    tok = lambda w: pl.BlockSpec((1, tm, w), lambda bi, i: (bi, i, 0))
    return pl.pallas_call(
        _router_kernel,
        grid=(b, nbs),
        in_specs=[
            tok(d),
            pl.BlockSpec((1, 1, d), lambda bi, i: (bi, 0, 0)),
            pl.BlockSpec((1, 1, d), lambda bi, i: (bi, 0, 0)),
            pl.BlockSpec((d, LANES), lambda bi, i: (0, 0)),
        ],
        out_specs=[
            tok(d), tok(LANES), tok(LANES),
            pl.BlockSpec((1, SUBLANES, tm), lambda bi, i: (bi * nbs + i, 0, 0)),
            pl.BlockSpec((1, 1, LANES), lambda bi, i: (bi * nbs + i, 0, 0)),
        ],
        out_shape=[
            jax.ShapeDtypeStruct((b, s, d), BF16),
            jax.ShapeDtypeStruct((b, s, LANES), F32),
            jax.ShapeDtypeStruct((b, s, LANES), F32),
            jax.ShapeDtypeStruct((b * nbs, SUBLANES, tm), F32),
            jax.ShapeDtypeStruct((b * nbs, 1, LANES), F32),
        ],
        compiler_params=_cparams("parallel", "parallel"),
        name="moe_router",
    )(x, sc, sh, w_router_pad)


def _moe_dispatch_kernel(blk_ref, exp_ref, sub_ref, dst_ref, n_ref, h_ref, post_ref, zero_ref, o_ref):
    del zero_ref
    s = pl.program_id(0)

    @pl.when(s < n_ref[0])
    def _():
        rows = o_ref.shape[0]
        pos_row = post_ref[0, pl.ds(exp_ref[s], 1), :]
        want = sub_ref[s] * rows + lax.broadcasted_iota(jnp.int32, (rows, 1), 0)
        onehot = jnp.where(pos_row == want.astype(F32), 1.0, 0.0).astype(BF16)
        o_ref[...] = jnp.dot(onehot, h_ref[...], preferred_element_type=F32).astype(o_ref.dtype)


def _moe_dispatch(lists, h2d, post, zeros, *, bi, rows):
    blk, exp, sub, dst, n = lists
    d = h2d.shape[1]
    grid_spec = pltpu.PrefetchScalarGridSpec(
        num_scalar_prefetch=5,
        grid=(blk.shape[0],),
        in_specs=[
            pl.BlockSpec((bi, d), lambda s, blk, exp, sub, dst, n: (blk[s], 0)),
            pl.BlockSpec((1, SUBLANES, bi), lambda s, blk, exp, sub, dst, n: (blk[s], 0, 0)),
            pl.BlockSpec(memory_space=pl.ANY),
        ],
        out_specs=pl.BlockSpec((rows, d), lambda s, blk, exp, sub, dst, n: (dst[s], 0)),
    )
    return pl.pallas_call(
        _moe_dispatch_kernel,
        grid_spec=grid_spec,
        out_shape=jax.ShapeDtypeStruct(zeros.shape, zeros.dtype),
        input_output_aliases={7: 0},
        compiler_params=_cparams("arbitrary"),
        name="moe_dispatch",
    )(blk, exp, sub, dst, n, h2d, post, zeros)


def _moe_expert_kernel(exp_ref, tile_ref, n_ref, xg_ref, wg_ref, wu_ref, wd_ref, o_ref, *, n_split):
    del exp_ref, tile_ref
    s = pl.program_id(0)

    @pl.when(s < n_ref[0])
    def _():
        xg = xg_ref[...]
        ff = wg_ref.shape[2]
        tf = ff // n_split
        acc = jnp.zeros(o_ref.shape, F32)
        for f in range(n_split):
            fs = slice(f * tf, (f + 1) * tf)
            g = jnp.dot(xg, wg_ref[0, :, fs], preferred_element_type=F32)
            u = jnp.dot(xg, wu_ref[0, :, fs], preferred_element_type=F32)
            acc = acc + jnp.dot((_silu(g) * u).astype(BF16), wd_ref[0, fs, :], preferred_element_type=F32)
        o_ref[...] = acc.astype(o_ref.dtype)


def _moe_experts(exp2, tile2, n2, xg, wg, wu, wd, *, rows2):
    _, d, ff = wg.shape
    grid_spec = pltpu.PrefetchScalarGridSpec(
        num_scalar_prefetch=3,
        grid=(exp2.shape[0],),
        in_specs=[
            pl.BlockSpec((rows2, d), lambda s, exp, tile, n: (tile[s], 0)),
            pl.BlockSpec((1, d, ff), lambda s, exp, tile, n: (exp[s], 0, 0)),
            pl.BlockSpec((1, d, ff), lambda s, exp, tile, n: (exp[s], 0, 0)),
            pl.BlockSpec((1, ff, d), lambda s, exp, tile, n: (exp[s], 0, 0)),
        ],
        out_specs=pl.BlockSpec((rows2, d), lambda s, exp, tile, n: (tile[s], 0)),
    )
    return pl.pallas_call(
        functools.partial(_moe_expert_kernel, n_split=2),
        grid_spec=grid_spec,
        out_shape=jax.ShapeDtypeStruct(xg.shape, BF16),
        compiler_params=_cparams("arbitrary"),
        name="moe_experts",
    )(exp2, tile2, n2, xg, wg, wu, wd)


def _moe_combine_kernel(blk_ref, exp_ref, sub_ref, dst_ref, first_ref, last_ref, n_ref,
                        y_ref, pos_ref, comb_ref, x_ref, gate_ref, lng_ref, lnb_ref, o_ref, acc_ref):
    del blk_ref, dst_ref
    s = pl.program_id(0)

    @pl.when(s < n_ref[0])
    def _():
        @pl.when(first_ref[s] == 1)
        def _():
            acc_ref[...] = jnp.zeros(acc_ref.shape, F32)

        rows = y_ref.shape[0]
        e = exp_ref[s]
        pos = pos_ref[...]
        lane = lax.broadcasted_iota(jnp.int32, pos.shape, 1)
        pos_e = jnp.max(jnp.where(lane == e, pos, -2.0), axis=1, keepdims=True)
        w_e = jnp.sum(jnp.where(lane == e, comb_ref[...], 0.0), axis=1, keepdims=True)
        have = sub_ref[s] * rows + lax.broadcasted_iota(jnp.int32, (1, rows), 1)
        scatter = jnp.where(pos_e == have.astype(F32), w_e, 0.0).astype(BF16)
        acc_ref[...] += jnp.dot(scatter, y_ref[...], preferred_element_type=F32)

        @pl.when(last_ref[s] == 1)
        def _():
            r = ALPHA * x_ref[...] + (1.0 + gate_ref[0]) * acc_ref[...]
            o_ref[...] = _layer_norm(r) * lng_ref[...] + lnb_ref[...]


def _moe_combine(lists, y, pos2d, comb2d, x2d, gate, lng, lnb, *, bi, rows, blocks_per_batch):
    blk, exp, sub, dst, first, last, n = lists
    d = x2d.shape[1]
    by_blk = lambda w: pl.BlockSpec((bi, w), lambda s, blk, *_: (blk[s], 0))
    grid_spec = pltpu.PrefetchScalarGridSpec(
        num_scalar_prefetch=7,
        grid=(blk.shape[0],),
        in_specs=[
            pl.BlockSpec((rows, d), lambda s, blk, exp, sub, dst, *_: (dst[s], 0)),
            by_blk(LANES), by_blk(LANES), by_blk(d),
            pl.BlockSpec((1, 1, d), lambda s, blk, *_: (blk[s] // blocks_per_batch, 0, 0)),
            pl.BlockSpec((1, d), lambda s, *_: (0, 0)),
            pl.BlockSpec((1, d), lambda s, *_: (0, 0)),
        ],
        out_specs=by_blk(d),
        scratch_shapes=[pltpu.VMEM((bi, d), F32)],
    )
    return pl.pallas_call(
        _moe_combine_kernel,
        grid_spec=grid_spec,
        out_shape=jax.ShapeDtypeStruct(x2d.shape, F32),
        compiler_params=_cparams("arbitrary"),
        name="moe_combine",
    )(blk, exp, sub, dst, first, last, n, y, pos2d, comb2d, x2d, gate, lng, lnb)


def _moe_tile_lists(cnt, *, n_max, n2_max, rows, group):
    nblk, ne = cnt.shape
    i32 = jnp.int32
    nt = (cnt + rows - 1) // rows
    nt_e_pad = (nt.sum(axis=0) + group - 1) // group * group
    e_end = jnp.cumsum(nt_e_pad)
    dst0 = ((e_end - nt_e_pad)[None, :] + jnp.cumsum(nt, axis=0) - nt).reshape(-1)
    flat = nt.reshape(-1)
    ends = jnp.cumsum(flat)
    n_tiles = ends[-1]
    slot = jnp.minimum(jnp.arange(n_max, dtype=i32), n_tiles - 1)
    pair = jnp.searchsorted(ends, slot, side="right").astype(i32)
    blk = pair // ne
    sub = slot - (ends[pair] - flat[pair])
    blk_end = ends.reshape(nblk, ne)[:, -1]
    blk_begin = blk_end - nt.sum(axis=1)
    first = (slot == blk_begin[blk]).astype(i32)
    last = (slot == blk_end[blk] - 1).astype(i32)
    lists = (blk.astype(i32), (pair % ne).astype(i32), sub.astype(i32), (dst0[pair] + sub).astype(i32),
             first, last, n_tiles.astype(i32)[None])
    n2 = e_end[-1] // group
    tile2 = jnp.minimum(jnp.arange(n2_max, dtype=i32), n2 - 1)
    exp2 = jnp.minimum(jnp.searchsorted(e_end, tile2 * group, side="right"), ne - 1).astype(i32)
    return lists, (exp2, tile2.astype(i32), n2.astype(i32)[None])


def _tile(s, pref):
    return min(pref, s)


def _ssd_layer(x, sc, sh, gate, lng, lnb, w_in, conv_w, conv_b, dt_bias, a_log, d_skip, norm_w, w_out):
    s = x.shape[1]
    n_heads = dt_bias.shape[0]
    d_inner = n_heads * SSD_HEAD_DIM
    n_zxbc = w_in.shape[1] - n_heads
    pad = LANES - n_heads
    w_zxbc = w_in[:, :n_zxbc].astype(BF16)
    w_dt = jnp.pad(w_in[:, n_zxbc:], ((0, 0), (0, pad))).astype(BF16)
    zxbc, dt_raw = _ssd_in_proj(x, sc, sh, w_zxbc, w_dt, tm=_tile(s, 1024), tn=d_inner)
    yg = _ssd_scan(
        zxbc, dt_raw, conv_w, conv_b[None, :],
        jnp.pad(dt_bias, (0, pad))[None, :], jnp.pad(a_log, (0, pad))[None, :],
        jnp.repeat(d_skip, SSD_HEAD_DIM)[None, :], norm_w[None, :])
    return _mm_res_ln(yg, w_out.astype(BF16), x, gate, lng, lnb, tm=_tile(s, 512), name="ssd_out_proj")


def _mla_layer(x, positions, sc, sh, gate, lng, lnb, w_in, q_norm, kv_norm, w_uq, w_ukv, w_out):
    s = x.shape[1]
    d = x.shape[2]
    nh, hp = MLA_N_HEADS, MLA_HEAD_PAD
    qk = MLA_NOPE + MLA_ROPE
    rope_lo = MLA_NOPE
    w_cq_ckv = w_in[:, :MLA_Q_RANK + MLA_KV_RANK]
    w_kr = jnp.pad(w_in[:, MLA_Q_RANK + MLA_KV_RANK:], ((0, 0), (rope_lo, hp - qk)))
    w_in_pad = jnp.concatenate([w_cq_ckv, w_kr], axis=1).astype(BF16)
    cin = _mod_matmul(x, sc, sh, w_in_pad, tm=_tile(s, 1024), tn=MLA_IN_PAD, out_dtype=F32, name="mla_in_proj")
    w_uq_pad = jnp.pad(w_uq.reshape(MLA_Q_RANK, nh, qk), ((0, 0), (0, 0), (0, hp - qk)))
    w_uq_pad = w_uq_pad.reshape(MLA_Q_RANK, nh * hp).astype(BF16)
    w_ukv3 = w_ukv.reshape(MLA_KV_RANK, nh, MLA_NOPE + MLA_V)
    w_uk_pad = jnp.pad(w_ukv3[:, :, :MLA_NOPE], ((0, 0), (0, 0), (0, hp - MLA_NOPE)))
    w_uk_pad = w_uk_pad.reshape(MLA_KV_RANK, nh * hp).astype(BF16)
    w_uv = jnp.pad(w_ukv3[:, :, MLA_NOPE:], ((0, 0), (0, 0), (0, hp - MLA_V)))
    w_uv = w_uv.reshape(MLA_KV_RANK, nh * hp).astype(BF16)
    half = MLA_ROPE // 2
    freqs = ROPE_THETA ** (-jnp.arange(half, dtype=F32) / half)
    zeros = lambda n: jnp.zeros((n,), F32)
    freq_row = jnp.concatenate([zeros(rope_lo), freqs, freqs, zeros(hp - qk)])[None, :]
    sign_row = jnp.concatenate([zeros(rope_lo), -jnp.ones((half,), F32), jnp.ones((half,), F32),
                                zeros(hp - qk)])[None, :]
    q, k, v = _mla_proj(cin, positions[:, :, None], freq_row, sign_row, q_norm[None, :], kv_norm[None, :],
                        w_uq_pad, w_uk_pad, w_uv, tm=_tile(s, 512))
    attn = _attention(q, k, v, tq=_tile(s, 512))
    return _mm_res_ln(attn, w_out.astype(BF16), x, gate, lng, lnb, tm=_tile(s, 512), name="mla_out_proj")


def _sg_layer(x, sc, sh, gate, lng, lnb, w_in, b_in, ln_g, ln_b, w_s, b_s, w_out):
    s = x.shape[1]
    uv = _sg_in_proj(x, sc, sh, w_in.astype(BF16), b_in[None, :], ln_g[None, :], ln_b[None, :], tm=_tile(s, 512))
    return _sg_out(uv, w_s, b_s.T, w_out.astype(BF16), x, gate, lng, lnb, tm=_tile(s, 512))


def _dense_ffn(x, sc, sh, gate, lng, lnb, w_gate, w_up, w_down):
    s = x.shape[1]
    ff = w_gate.shape[1]
    return _swiglu_res_ln(x, sc, sh, w_gate.astype(BF16), w_up.astype(BF16), w_down.astype(BF16),
                          gate, lng, lnb, tm=_tile(s, 512), tf=ff // 2)


def _moe_ffn(x, sc, sh, gate, lng, lnb, w_router, w_gate, w_up, w_down):
    b, s, d = x.shape
    ne = w_router.shape[1]
    bi = _tile(s, MOE_BLOCK)
    nblk = b * (s // bi)
    tokens = b * s
    n_max = 2 * tokens // MOE_ROWS + nblk * ne
    n2_max = (n_max + ne * (MOE_GROUP - 1) + MOE_GROUP - 1) // MOE_GROUP
    w_router_pad = jnp.pad(w_router, ((0, 0), (0, LANES - ne)))
    h, comb, pos, post, cnt = _router(x, sc, sh, w_router_pad, tm=bi)
    lists, lists2 = _moe_tile_lists(cnt[:, 0, :ne].astype(jnp.int32), n_max=n_max, n2_max=n2_max,
                                    rows=MOE_ROWS, group=MOE_GROUP)
    blk, exp, sub, dst, first, last, n = lists
    zeros = jnp.zeros((n2_max * MOE_GROUP * MOE_ROWS, d), BF16)
    xg = _moe_dispatch((blk, exp, sub, dst, n), h.reshape(tokens, d), post, zeros, bi=bi, rows=MOE_ROWS)
    y = _moe_experts(*lists2, xg, w_gate.astype(BF16), w_up.astype(BF16), w_down.astype(BF16),
                     rows2=MOE_GROUP * MOE_ROWS)
    out = _moe_combine(lists, y, pos.reshape(tokens, LANES), comb.reshape(tokens, LANES), x.reshape(tokens, d),
                       gate, lng, lnb, bi=bi, rows=MOE_ROWS, blocks_per_batch=s // bi)
    return out.reshape(b, s, d)


def kernel(x, c, positions, ada_w, ada_b, ln_g, ln_b, ssd_w_in, ssd_conv_w, ssd_conv_b, ssd_dt_bias, ssd_a_log, ssd_d_skip, ssd_norm_w, ssd_w_out, mla_w_in, mla_q_norm, mla_kv_norm, mla_w_uq, mla_w_ukv, mla_w_out, sg_w_in, sg_b_in, sg_ln_g, sg_ln_b, sg_w_s, sg_b_s, sg_w_out, ffn_w_gate, ffn_w_up, ffn_w_down, moe_w_router, moe_w_gate, moe_w_up, moe_w_down):
    batch = x.shape[0]
    depth = ada_w.shape[0]
    c_pad = jnp.pad(c, ((0, SUBLANES - batch), (0, 0)))
    mod = _ada_mod(c_pad, ada_w, ada_b)[:, :, :batch]
    for i in range(depth):
        sh_m, sc_m, g_m, sh_f, sc_f, g_f = [mod[i, t][:, None, :] for t in range(6)]
        lng_m, lnb_m = ln_g[i, 0][None, :], ln_b[i, 0][None, :]
        lng_f, lnb_f = ln_g[i, 1][None, :], ln_b[i, 1][None, :]
        kind, j = i % 3, i // 3
        if kind == 0:
            x = _ssd_layer(x, sc_m, sh_m, g_m, lng_m, lnb_m, ssd_w_in[j], ssd_conv_w[j], ssd_conv_b[j],
                           ssd_dt_bias[j], ssd_a_log[j], ssd_d_skip[j], ssd_norm_w[j], ssd_w_out[j])
        elif kind == 1:
            x = _mla_layer(x, positions, sc_m, sh_m, g_m, lng_m, lnb_m, mla_w_in[j], mla_q_norm[j],
                           mla_kv_norm[j], mla_w_uq[j], mla_w_ukv[j], mla_w_out[j])
        else:
            x = _sg_layer(x, sc_m, sh_m, g_m, lng_m, lnb_m, sg_w_in[j], sg_b_in[j], sg_ln_g[j], sg_ln_b[j],
                          sg_w_s[j], sg_b_s[j], sg_w_out[j])
        k = i // 2
        if i % 2 == 0:
            x = _dense_ffn(x, sc_f, sh_f, g_f, lng_f, lnb_f, ffn_w_gate[k], ffn_w_up[k], ffn_w_down[k])
        else:
            x = _moe_ffn(x, sc_f, sh_f, g_f, lng_f, lnb_f, moe_w_router[k], moe_w_gate[k], moe_w_up[k],
                         moe_w_down[k])
    return x
```

```python
import functools
import math

import jax
import jax.numpy as jnp
from jax import lax
from jax.experimental import pallas as pl
from jax.experimental.pallas import tpu as pltpu

F32 = jnp.float32
BF16 = jnp.bfloat16
HIGHEST = lax.Precision.HIGHEST

DEPTH = 4
ALPHA = (2.0 * DEPTH) ** 0.25
LN_EPS = 1e-5
RMS_EPS = 1e-6
ROPE_THETA = 10000.0

LANES = 128
SUBLANES = 8
VMEM_LIMIT = 56 * 1024 * 1024

SSD_HEAD_DIM = 64
SSD_N_GROUPS = 8
SSD_HPG = 4
SSD_D_STATE = 128
SSD_CONV = 4
SSD_CHUNK = 128
SSD_GROUP_W = SSD_HPG * SSD_HEAD_DIM

MLA_N_HEADS = 16
MLA_NOPE = 64
MLA_ROPE = 32
MLA_V = 64
MLA_Q_RANK = 512
MLA_KV_RANK = 256
MLA_HEAD_PAD = 128
MLA_IN_PAD = MLA_Q_RANK + MLA_KV_RANK + MLA_HEAD_PAD
QK_SCALE = (MLA_NOPE + MLA_ROPE) ** -0.5
LOG2E = 1.4426950408889634
NEG_BIG = -1e30

SG_GROUPS = 8
SG_CHUNK = 128

N_EXPERTS = 8
MOE_BLOCK = 1024
MOE_ROWS = 128
MOE_GROUP = 4


def _cparams(*sem):
    return pltpu.CompilerParams(dimension_semantics=sem, vmem_limit_bytes=VMEM_LIMIT)


def _layer_norm(r):
    mu = jnp.mean(r, axis=-1, keepdims=True)
    d = r - mu
    var = jnp.mean(d * d, axis=-1, keepdims=True)
    return d * lax.rsqrt(var + LN_EPS)


def _silu(v):
    return v * jax.nn.sigmoid(v)


def _modulate(x_ref, sc_ref, sh_ref):
    return x_ref[0] * (1.0 + sc_ref[0]) + sh_ref[0]


def _res_ln(x_ref, gate_ref, y, lng_ref, lnb_ref):
    r = ALPHA * x_ref[0] + (1.0 + gate_ref[0]) * y
    return _layer_norm(r) * lng_ref[...] + lnb_ref[...]


def _ada_kernel(c_ref, w_ref, b_ref, o_ref):
    cond = _silu(c_ref[...])
    o_ref[0, 0] = jnp.dot(cond, w_ref[0], preferred_element_type=F32, precision=HIGHEST) + b_ref[0, 0]


def _ada_mod(c_pad, ada_w, ada_b):
    depth, d, _ = ada_w.shape
    rows = c_pad.shape[0]
    return pl.pallas_call(
        _ada_kernel,
        grid=(depth, 6),
        in_specs=[
            pl.BlockSpec((rows, d), lambda i, j: (0, 0)),
            pl.BlockSpec((1, d, d), lambda i, j: (i, 0, j)),
            pl.BlockSpec((1, 1, 1, d), lambda i, j: (i, j, 0, 0)),
        ],
        out_specs=pl.BlockSpec((1, 1, rows, d), lambda i, j: (i, j, 0, 0)),
        out_shape=jax.ShapeDtypeStruct((depth, 6, rows, d), F32),
        compiler_params=_cparams("arbitrary", "arbitrary"),
        name="ada_mod",
    )(c_pad, ada_w, ada_b.reshape(depth, 6, 1, d))


def _modmm_kernel(x_ref, sc_ref, sh_ref, w_ref, o_ref, h_ref):
    @pl.when(pl.program_id(2) == 0)
    def _():
        h_ref[...] = _modulate(x_ref, sc_ref, sh_ref).astype(BF16)

    o_ref[0] = jnp.dot(h_ref[...], w_ref[...], preferred_element_type=F32).astype(o_ref.dtype)


def _mod_matmul(x, sc, sh, w, *, tm, tn, out_dtype, name):
    b, s, d = x.shape
    n = w.shape[1]
    return pl.pallas_call(
        _modmm_kernel,
        grid=(b, s // tm, n // tn),
        in_specs=[
            pl.BlockSpec((1, tm, d), lambda bi, i, j: (bi, i, 0)),
            pl.BlockSpec((1, 1, d), lambda bi, i, j: (bi, 0, 0)),
            pl.BlockSpec((1, 1, d), lambda bi, i, j: (bi, 0, 0)),
            pl.BlockSpec((d, tn), lambda bi, i, j: (0, j)),
        ],
        out_specs=pl.BlockSpec((1, tm, tn), lambda bi, i, j: (bi, i, j)),
        out_shape=jax.ShapeDtypeStruct((b, s, n), out_dtype),
        scratch_shapes=[pltpu.VMEM((tm, d), BF16)],
        compiler_params=_cparams("parallel", "parallel", "arbitrary"),
        name=name,
    )(x, sc, sh, w)


def _ssd_in_kernel(x_ref, sc_ref, sh_ref, w_ref, wdt_ref, o_ref, dt_ref, h_ref):
    @pl.when(pl.program_id(2) == 0)
    def _():
        h_ref[...] = _modulate(x_ref, sc_ref, sh_ref).astype(BF16)
        dt_ref[0] = jnp.dot(h_ref[...], wdt_ref[...], preferred_element_type=F32)

    o_ref[0] = jnp.dot(h_ref[...], w_ref[...], preferred_element_type=F32).astype(o_ref.dtype)


def _ssd_in_proj(x, sc, sh, w_zxbc, w_dt, *, tm, tn):
    b, s, d = x.shape
    n = w_zxbc.shape[1]
    return pl.pallas_call(
        _ssd_in_kernel,
        grid=(b, s // tm, n // tn),
        in_specs=[
            pl.BlockSpec((1, tm, d), lambda bi, i, j: (bi, i, 0)),
            pl.BlockSpec((1, 1, d), lambda bi, i, j: (bi, 0, 0)),
            pl.BlockSpec((1, 1, d), lambda bi, i, j: (bi, 0, 0)),
            pl.BlockSpec((d, tn), lambda bi, i, j: (0, j)),
            pl.BlockSpec((d, LANES), lambda bi, i, j: (0, 0)),
        ],
        out_specs=[
            pl.BlockSpec((1, tm, tn), lambda bi, i, j: (bi, i, j)),
            pl.BlockSpec((1, tm, LANES), lambda bi, i, j: (bi, i, 0)),
        ],
        out_shape=[
            jax.ShapeDtypeStruct((b, s, n), BF16),
            jax.ShapeDtypeStruct((b, s, LANES), F32),
        ],
        scratch_shapes=[pltpu.VMEM((tm, d), BF16)],
        compiler_params=_cparams("parallel", "parallel", "arbitrary"),
        name="ssd_in_proj",
    )(x, sc, sh, w_zxbc, w_dt)


def _ssd_kernel(z_ref, x_ref, bc_ref, dt_ref, cwx_ref, cwbc_ref, cbx_ref, cbbc_ref, dtb_ref, alog_ref,
                dskip_ref, nw_ref, o_ref, prevx_ref, prevbc_ref, taps_ref, xs_ref, b_ref, c_ref, xh_ref,
                acum_ref, key_ref, wend_ref, state_ref):
    L = SSD_CHUNK
    gw = SSD_GROUP_W
    n_st = SSD_D_STATE
    slab = 2 * SUBLANES
    ci = pl.program_id(1)

    @pl.when(ci == 0)
    def _():
        prevx_ref[...] = jnp.zeros(prevx_ref.shape, BF16)
        prevbc_ref[...] = jnp.zeros(prevbc_ref.shape, BF16)
        state_ref[...] = jnp.zeros(state_ref.shape, F32)

    t_idx = lax.broadcasted_iota(jnp.int32, (L, SSD_CONV * L), 0)
    c_idx = lax.broadcasted_iota(jnp.int32, (L, SSD_CONV * L), 1)
    shift = (SSD_CONV - 1) - c_idx // L
    delta = t_idx - c_idx % L
    rot = jnp.where((delta == shift) | (delta == shift - L), 1.0, 0.0).astype(BF16)
    slab_row = lax.broadcasted_iota(jnp.int32, (slab, x_ref.shape[2]), 0)

    def conv(cur_ref, prev_ref, cw_ref, cb_ref):
        body = cur_ref[0, 0:L - slab, :]
        last = cur_ref[0, L - slab:L, :]
        prev = prev_ref[...]
        for k in range(SSD_CONV):
            w_k = cw_ref[k:k + 1, :].astype(BF16)
            swapped = jnp.where(slab_row >= slab - (SSD_CONV - 1 - k), prev, last)
            taps_ref[k * L:(k + 1) * L - slab, :] = body * w_k
            taps_ref[(k + 1) * L - slab:(k + 1) * L, :] = swapped * w_k
        prev_ref[...] = last
        return _silu(jnp.dot(rot, taps_ref[...], preferred_element_type=F32) + cb_ref[...])

    xs = conv(x_ref, prevx_ref, cwx_ref, cbx_ref)
    xs_ref[...] = xs
    xh_ref[...] = xs.astype(BF16)
    bc = conv(bc_ref, prevbc_ref, cwbc_ref, cbbc_ref)
    half = bc.shape[1] // 2
    b_ref[...] = bc[:, :half]
    c_ref[...] = bc[:, half:].astype(BF16)

    dtv = dt_ref[0] + dtb_ref[...]
    dt = jnp.maximum(dtv, 0.0) + jnp.log1p(jnp.exp(-jnp.abs(dtv)))
    a_neg = -jnp.exp(alog_ref[...])
    row = lax.broadcasted_iota(jnp.int32, (L, L), 0)
    col = lax.broadcasted_iota(jnp.int32, (L, L), 1)
    causal = col <= row
    tri = jnp.where(causal, 1.0, 0.0).astype(F32)
    a_cum = jnp.dot(tri, dt * (a_neg * LOG2E), preferred_element_type=F32, precision=HIGHEST)
    a_cum_t = a_cum.T
    key_t = a_cum_t - jnp.log2(dt.T)
    w_end_t = jnp.exp2(a_cum_t[:, L - 1:L] - key_t)

    acum_ref[...] = a_cum
    key_ref[...] = key_t
    wend_ref[...] = w_end_t

    def group(g):
        lane = lax.broadcasted_iota(jnp.int32, (L, LANES), 1)
        head_of_lane = lax.broadcasted_iota(jnp.int32, (L, gw), 1) // SSD_HEAD_DIM
        causal = lax.broadcasted_iota(jnp.int32, (L, L), 1) <= lax.broadcasted_iota(jnp.int32, (L, L), 0)
        sl = slice(g * gw, (g + 1) * gw)
        sn = slice(g * n_st, (g + 1) * n_st)
        c_g = c_ref[:, sn]
        b_g32 = b_ref[:, sn]
        b_gt = b_g32.T
        cb = lax.dot_general(c_g, b_g32.astype(BF16), (((1,), (1,)), ((), ())), preferred_element_type=F32)
        st = state_ref[g]
        y_off = jnp.dot(c_g, st.astype(BF16), preferred_element_type=F32)
        xh_g = xh_ref[:, sl]
        m_heads, bw_heads, x_heads, ea_heads = [], [], [], []
        for r in range(SSD_HPG):
            h = SSD_HPG * g + r
            a_col = jnp.broadcast_to(acum_ref[:, h:h + 1], (L, L))
            decay_dt = jnp.exp2(jnp.where(causal, a_col - key_ref[h:h + 1, :], NEG_BIG))
            m_heads.append((cb * decay_dt).astype(BF16))
            bw_heads.append((b_gt * wend_ref[h:h + 1, :]).astype(BF16))
            x_heads.append(jnp.where(head_of_lane == r, xh_g, jnp.zeros_like(xh_g)))
            ea_heads.append(jnp.exp2(a_col))
        x_stack = jnp.concatenate(x_heads, axis=0)
        y_diag = jnp.dot(jnp.concatenate(m_heads, axis=1), x_stack, preferred_element_type=F32)
        new_st = jnp.dot(jnp.concatenate(bw_heads, axis=1), x_stack, preferred_element_type=F32)
        ea_g = jnp.concatenate([jnp.where(lane < SSD_HEAD_DIM, ea_heads[0], ea_heads[1]),
                                jnp.where(lane < SSD_HEAD_DIM, ea_heads[2], ea_heads[3])], axis=1)
        y = y_diag + y_off * ea_g + xs_ref[:, sl] * dskip_ref[:, sl]
        z_g = z_ref[0, :, sl].astype(F32)
        t = y * _silu(z_g)
        ms = jnp.mean(t * t, axis=-1, keepdims=True)
        o_ref[0, :, sl] = (t * lax.rsqrt(ms + RMS_EPS) * nw_ref[:, sl]).astype(o_ref.dtype)
        state_ref[g] = st * ea_g[L - 1:L, :] + new_st

    for g in range(SSD_N_GROUPS):
        group(g)


def _ssd_scan(zxbc, dt_raw, conv_w, conv_b, dt_bias, a_log, d_skip, norm_w):
    b, s, _ = zxbc.shape
    L = SSD_CHUNK
    di = SSD_N_GROUPS * SSD_GROUP_W
    blk = lambda j: pl.BlockSpec((1, L, di), lambda bi, ci, j=j: (bi, ci, j))
    vec = lambda w: pl.BlockSpec((1, w), lambda bi, ci: (0, 0))
    cw = lambda j: pl.BlockSpec((SSD_CONV, di), lambda bi, ci, j=j: (0, j))
    cb = lambda j: pl.BlockSpec((1, di), lambda bi, ci, j=j: (0, j))
    return pl.pallas_call(
        _ssd_kernel,
        grid=(b, s // L),
        in_specs=[
            blk(0), blk(1), blk(2),
            pl.BlockSpec((1, L, LANES), lambda bi, ci: (bi, ci, 0)),
            cw(0), cw(1), cb(0), cb(1),
            vec(LANES), vec(LANES), vec(di), vec(di),
        ],
        out_specs=pl.BlockSpec((1, L, di), lambda bi, ci: (bi, ci, 0)),
        out_shape=jax.ShapeDtypeStruct((b, s, di), BF16),
        scratch_shapes=[
            pltpu.VMEM((2 * SUBLANES, di), BF16),
            pltpu.VMEM((2 * SUBLANES, di), BF16),
            pltpu.VMEM((SSD_CONV * L, di), BF16),
            pltpu.VMEM((L, di), F32),
            pltpu.VMEM((L, di // 2), F32),
            pltpu.VMEM((L, di // 2), BF16),
            pltpu.VMEM((L, di), BF16),
            pltpu.VMEM((L, LANES), F32),
            pltpu.VMEM((LANES, L), F32),
            pltpu.VMEM((LANES, L), F32),
            pltpu.VMEM((SSD_N_GROUPS, SSD_D_STATE, SSD_GROUP_W), F32),
        ],
        compiler_params=_cparams("arbitrary", "arbitrary"),
        name="ssd_scan",
    )(zxbc, zxbc, zxbc, dt_raw, conv_w, conv_w, conv_b, conv_b, dt_bias, a_log, d_skip, norm_w)


def _mm_res_ln_kernel(a_ref, w_ref, x_ref, gate_ref, lng_ref, lnb_ref, o_ref):
    y = jnp.dot(a_ref[0], w_ref[...], preferred_element_type=F32)
    o_ref[0] = _res_ln(x_ref, gate_ref, y, lng_ref, lnb_ref)


def _mm_res_ln(a, w, x, gate, lng, lnb, *, tm, name):
    b, s, d = x.shape
    k = a.shape[2]
    return pl.pallas_call(
        _mm_res_ln_kernel,
        grid=(b, s // tm),
        in_specs=[
            pl.BlockSpec((1, tm, k), lambda bi, i: (bi, i, 0)),
            pl.BlockSpec((k, d), lambda bi, i: (0, 0)),
            pl.BlockSpec((1, tm, d), lambda bi, i: (bi, i, 0)),
            pl.BlockSpec((1, 1, d), lambda bi, i: (bi, 0, 0)),
            pl.BlockSpec((1, d), lambda bi, i: (0, 0)),
            pl.BlockSpec((1, d), lambda bi, i: (0, 0)),
        ],
        out_specs=pl.BlockSpec((1, tm, d), lambda bi, i: (bi, i, 0)),
        out_shape=jax.ShapeDtypeStruct((b, s, d), F32),
        compiler_params=_cparams("parallel", "parallel"),
        name=name,
    )(a, w, x, gate, lng, lnb)


def _mla_proj_kernel(cin_ref, pos_ref, freq_ref, sign_ref, qn_ref, kvn_ref, wuq_ref, wuk_ref, wuv_ref,
                     q_ref, k_ref, v_ref):
    cin = cin_ref[0]
    tm = cin.shape[0]
    hp = MLA_HEAD_PAD

    def rms(v, w_ref):
        return v * lax.rsqrt(jnp.mean(v * v, axis=-1, keepdims=True) + RMS_EPS) * w_ref[...]

    cq = rms(cin[:, :MLA_Q_RANK], qn_ref).astype(BF16)
    ckv = rms(cin[:, MLA_Q_RANK:MLA_Q_RANK + MLA_KV_RANK], kvn_ref).astype(BF16)
    k_rope = cin[:, MLA_Q_RANK + MLA_KV_RANK:]
    q = jnp.dot(cq, wuq_ref[...], preferred_element_type=F32)
    k = jnp.dot(ckv, wuk_ref[...], preferred_element_type=F32)
    ones_col = jnp.where(lax.broadcasted_iota(jnp.int32, (1, wuv_ref.shape[1]), 1) % hp == MLA_V, 1.0, 0.0)
    v_ref[0] = (jnp.dot(ckv, wuv_ref[...], preferred_element_type=F32) + ones_col).astype(v_ref.dtype)

    ang = pos_ref[0].astype(F32) * freq_ref[...]
    cos = jnp.cos(ang)
    sin = jnp.sin(ang) * sign_ref[...]
    lane = lax.broadcasted_iota(jnp.int32, (tm, hp), 1)
    first_half = lane < MLA_NOPE + MLA_ROPE // 2

    def rope(xh):
        swapped = jnp.where(first_half,
                            pltpu.roll(xh, hp - MLA_ROPE // 2, 1),
                            pltpu.roll(xh, MLA_ROPE // 2, 1))
        return xh * cos + swapped * sin

    kr = rope(k_rope)
    for h in range(MLA_N_HEADS):
        sl = slice(h * hp, (h + 1) * hp)
        q_ref[0, :, sl] = (rope(q[:, sl]) * (QK_SCALE * LOG2E)).astype(q_ref.dtype)
        k_ref[0, :, sl] = (k[:, sl] + kr).astype(k_ref.dtype)


def _mla_proj(cin, pos, freq, sign, q_norm, kv_norm, w_uq, w_uk, w_uv, *, tm):
    b, s, n_in = cin.shape
    nq = w_uq.shape[1]
    nv = w_uv.shape[1]
    full = lambda a: pl.BlockSpec(a.shape, lambda bi, i: (0,) * a.ndim)
    return pl.pallas_call(
        _mla_proj_kernel,
        grid=(b, s // tm),
        in_specs=[
            pl.BlockSpec((1, tm, n_in), lambda bi, i: (bi, i, 0)),
            pl.BlockSpec((1, tm, 1), lambda bi, i: (bi, i, 0)),
            full(freq), full(sign), full(q_norm), full(kv_norm), full(w_uq), full(w_uk), full(w_uv),
        ],
        out_specs=[
            pl.BlockSpec((1, tm, nq), lambda bi, i: (bi, i, 0)),
            pl.BlockSpec((1, tm, nq), lambda bi, i: (bi, i, 0)),
            pl.BlockSpec((1, tm, nv), lambda bi, i: (bi, i, 0)),
        ],
        out_shape=[
            jax.ShapeDtypeStruct((b, s, nq), BF16),
            jax.ShapeDtypeStruct((b, s, nq), BF16),
            jax.ShapeDtypeStruct((b, s, nv), BF16),
        ],
        compiler_params=_cparams("parallel", "parallel"),
        name="mla_proj",
    )(cin, pos, freq, sign, q_norm, kv_norm, w_uq, w_uk, w_uv)


def _attn_kernel(q_ref, k_ref, v_ref, o_ref, acc_ref, s0_ref, s1_ref, *, tq, n_kb):
    qi = pl.program_id(2)
    hp = MLA_HEAD_PAD
    key_minus_query = (lax.broadcasted_iota(jnp.int32, (tq, tq), 1)
                       - lax.broadcasted_iota(jnp.int32, (tq, tq), 0))

    def scores(kb, s_ref):
        start = pl.multiple_of(jnp.minimum(kb, n_kb - 1) * tq, tq)
        keep = key_minus_query <= (qi - kb) * tq
        for hh in range(2):
            q = q_ref[0, :, hh * hp:(hh + 1) * hp]
            k = k_ref[0, pl.ds(start, tq), hh * hp:(hh + 1) * hp]
            s = lax.dot_general(q, k, (((1,), (1,)), ((), ())), preferred_element_type=F32)
            s_ref[hh] = jnp.where(keep, s, NEG_BIG)

    def softmax_pv(kb, s_ref, m_prev):
        start = pl.multiple_of(jnp.minimum(kb, n_kb - 1) * tq, tq)
        m_out = []
        for hh in range(2):
            v = v_ref[0, pl.ds(start, tq), hh * hp:(hh + 1) * hp]
            s = s_ref[hh]
            m_new = jnp.maximum(m_prev[hh], jnp.max(s, axis=1, keepdims=True))
            alpha = jnp.exp2(m_prev[hh] - m_new)
            p = jnp.exp2((s - m_new).astype(BF16))
            acc_ref[hh] = acc_ref[hh] * alpha + jnp.dot(p, v, preferred_element_type=F32)
            m_out.append(m_new)
        return tuple(m_out)

    acc_ref[...] = jnp.zeros(acc_ref.shape, F32)
    m_init = jnp.full((tq, 1), NEG_BIG, F32)
    scores(0, s0_ref)

    def two_blocks(t, m):
        m = softmax_pv(2 * t, s0_ref, m)
        scores(2 * t + 1, s1_ref)
        m = softmax_pv(2 * t + 1, s1_ref, m)
        scores(2 * t + 2, s0_ref)
        return m

    lax.fori_loop(0, (qi + 2) // 2, two_blocks, (m_init, m_init))
    out_a = acc_ref[0] / acc_ref[0][:, MLA_V:MLA_V + 1]
    out_b = acc_ref[1] / acc_ref[1][:, MLA_V:MLA_V + 1]
    lane = lax.broadcasted_iota(jnp.int32, (tq, hp), 1)
    o_ref[0] = jnp.where(lane < MLA_V, out_a, pltpu.roll(out_b, MLA_V, 1)).astype(o_ref.dtype)


def _attention(q, k, v, *, tq):
    b, s, _ = q.shape
    pairs = MLA_N_HEADS // 2
    return pl.pallas_call(
        functools.partial(_attn_kernel, tq=tq, n_kb=s // tq),
        grid=(b, pairs, s // tq),
        in_specs=[
            pl.BlockSpec((1, tq, 2 * MLA_HEAD_PAD), lambda bi, j, i: (bi, i, j)),
            pl.BlockSpec((1, s, 2 * MLA_HEAD_PAD), lambda bi, j, i: (bi, 0, j)),
            pl.BlockSpec((1, s, 2 * MLA_HEAD_PAD), lambda bi, j, i: (bi, 0, j)),
        ],
        out_specs=pl.BlockSpec((1, tq, 2 * MLA_V), lambda bi, j, i: (bi, i, j)),
        out_shape=jax.ShapeDtypeStruct((b, s, MLA_N_HEADS * MLA_V), BF16),
        scratch_shapes=[pltpu.VMEM((2, tq, MLA_HEAD_PAD), F32), pltpu.VMEM((2, tq, tq), F32),
                        pltpu.VMEM((2, tq, tq), F32)],
        compiler_params=_cparams("parallel", "parallel", "arbitrary"),
        name="mla_attention",
    )(q, k, v)


def _sg_in_kernel(x_ref, sc_ref, sh_ref, w_ref, b_ref, lng_ref, lnb_ref, o_ref, h_ref):
    j = pl.program_id(2)

    @pl.when(j == 0)
    def _():
        h_ref[...] = _modulate(x_ref, sc_ref, sh_ref).astype(BF16)

    y = jax.nn.gelu(jnp.dot(h_ref[...], w_ref[...], preferred_element_type=F32) + b_ref[...])

    @pl.when(j == 0)
    def _():
        o_ref[0] = y.astype(o_ref.dtype)

    @pl.when(j == 1)
    def _():
        o_ref[0] = (_layer_norm(y) * lng_ref[...] + lnb_ref[...]).astype(o_ref.dtype)


def _sg_in_proj(x, sc, sh, w, bias, lng, lnb, *, tm):
    b, s, d = x.shape
    n = w.shape[1]
    tn = n // 2
    return pl.pallas_call(
        _sg_in_kernel,
        grid=(b, s // tm, 2),
        in_specs=[
            pl.BlockSpec((1, tm, d), lambda bi, i, j: (bi, i, 0)),
            pl.BlockSpec((1, 1, d), lambda bi, i, j: (bi, 0, 0)),
            pl.BlockSpec((1, 1, d), lambda bi, i, j: (bi, 0, 0)),
            pl.BlockSpec((d, tn), lambda bi, i, j: (0, j)),
            pl.BlockSpec((1, tn), lambda bi, i, j: (0, j)),
            pl.BlockSpec((1, tn), lambda bi, i, j: (0, 0)),
            pl.BlockSpec((1, tn), lambda bi, i, j: (0, 0)),
        ],
        out_specs=pl.BlockSpec((1, tm, tn), lambda bi, i, j: (bi, i, j)),
        out_shape=jax.ShapeDtypeStruct((b, s, n), BF16),
        scratch_shapes=[pltpu.VMEM((tm, d), BF16)],
        compiler_params=_cparams("parallel", "parallel", "arbitrary"),
        name="sg_in_proj",
    )(x, sc, sh, w, bias, lng, lnb)


def _sg_out_kernel(u_ref, v_ref, ws_ref, bst_ref, w_ref, x_ref, gate_ref, lng_ref, lnb_ref, o_ref, gated_ref):
    tm = u_ref.shape[1]
    L = SG_CHUNK
    gd = u_ref.shape[2] // SG_GROUPS
    row = lax.broadcasted_iota(jnp.int32, (L, L), 0)
    col = lax.broadcasted_iota(jnp.int32, (L, L), 1)
    causal = col <= row
    for g in range(SG_GROUPS):
        ws = jnp.where(causal, ws_ref[g], 0.0).astype(BF16)
        bias = bst_ref[:, g:g + 1]
        for c in range(tm // L):
            rs = slice(c * L, (c + 1) * L)
            cs = slice(g * gd, (g + 1) * gd)
            mixed = jnp.dot(ws, v_ref[0, rs, cs], preferred_element_type=F32) + bias
            gated_ref[rs, cs] = (u_ref[0, rs, cs].astype(F32) * mixed).astype(BF16)
    y = jnp.dot(gated_ref[...], w_ref[...], preferred_element_type=F32)
    o_ref[0] = _res_ln(x_ref, gate_ref, y, lng_ref, lnb_ref)


def _sg_out(uv, w_s, b_s_t, w_out, x, gate, lng, lnb, *, tm):
    b, s, d = x.shape
    sgd = uv.shape[2] // 2
    full = lambda a: pl.BlockSpec(a.shape, lambda bi, i: (0,) * a.ndim)
    return pl.pallas_call(
        _sg_out_kernel,
        grid=(b, s // tm),
        in_specs=[
            pl.BlockSpec((1, tm, sgd), lambda bi, i: (bi, i, 0)),
            pl.BlockSpec((1, tm, sgd), lambda bi, i: (bi, i, 1)),
            full(w_s), full(b_s_t), full(w_out),
            pl.BlockSpec((1, tm, d), lambda bi, i: (bi, i, 0)),
            pl.BlockSpec((1, 1, d), lambda bi, i: (bi, 0, 0)),
            pl.BlockSpec((1, d), lambda bi, i: (0, 0)),
            pl.BlockSpec((1, d), lambda bi, i: (0, 0)),
        ],
        out_specs=pl.BlockSpec((1, tm, d), lambda bi, i: (bi, i, 0)),
        out_shape=jax.ShapeDtypeStruct((b, s, d), F32),
        scratch_shapes=[pltpu.VMEM((tm, sgd), BF16)],
        compiler_params=_cparams("parallel", "parallel"),
        name="sg_out",
    )(uv, uv, w_s, b_s_t, w_out, x, gate, lng, lnb)


def _swiglu_kernel(x_ref, sc_ref, sh_ref, wg_ref, wu_ref, wd_ref, gate_ref, lng_ref, lnb_ref, o_ref,
                   h_ref, acc_ref):
    f = pl.program_id(2)

    @pl.when(f == 0)
    def _():
        h_ref[...] = _modulate(x_ref, sc_ref, sh_ref).astype(BF16)
        acc_ref[...] = jnp.zeros(acc_ref.shape, F32)

    h = h_ref[...]
    g = jnp.dot(h, wg_ref[...], preferred_element_type=F32)
    u = jnp.dot(h, wu_ref[...], preferred_element_type=F32)
    acc_ref[...] += jnp.dot((_silu(g) * u).astype(BF16), wd_ref[...], preferred_element_type=F32)

    @pl.when(f == pl.num_programs(2) - 1)
    def _():
        o_ref[0] = _res_ln(x_ref, gate_ref, acc_ref[...], lng_ref, lnb_ref)


def _swiglu_res_ln(x, sc, sh, wg, wu, wd, gate, lng, lnb, *, tm, tf):
    b, s, d = x.shape
    ff = wg.shape[1]
    return pl.pallas_call(
        _swiglu_kernel,
        grid=(b, s // tm, ff // tf),
        in_specs=[
            pl.BlockSpec((1, tm, d), lambda bi, i, f: (bi, i, 0)),
            pl.BlockSpec((1, 1, d), lambda bi, i, f: (bi, 0, 0)),
            pl.BlockSpec((1, 1, d), lambda bi, i, f: (bi, 0, 0)),
            pl.BlockSpec((d, tf), lambda bi, i, f: (0, f)),
            pl.BlockSpec((d, tf), lambda bi, i, f: (0, f)),
            pl.BlockSpec((tf, d), lambda bi, i, f: (f, 0)),
            pl.BlockSpec((1, 1, d), lambda bi, i, f: (bi, 0, 0)),
            pl.BlockSpec((1, d), lambda bi, i, f: (0, 0)),
            pl.BlockSpec((1, d), lambda bi, i, f: (0, 0)),
        ],
        out_specs=pl.BlockSpec((1, tm, d), lambda bi, i, f: (bi, i, 0)),
        out_shape=jax.ShapeDtypeStruct((b, s, d), F32),
        scratch_shapes=[pltpu.VMEM((tm, d), BF16), pltpu.VMEM((tm, d), F32)],
        compiler_params=_cparams("parallel", "parallel", "arbitrary"),
        name="swiglu_res_ln",
    )(x, sc, sh, wg, wu, wd, gate, lng, lnb)


def _router_kernel(x_ref, sc_ref, sh_ref, wr_ref, h_ref, comb_ref, pos_ref, post_ref, cnt_ref):
    h = _modulate(x_ref, sc_ref, sh_ref)
    h_ref[0] = h.astype(h_ref.dtype)
    logits = jnp.dot(h, wr_ref[...], preferred_element_type=F32, precision=HIGHEST)
    tm = logits.shape[0]
    lane = lax.broadcasted_iota(jnp.int32, logits.shape, 1).astype(F32)
    lg = jnp.where(lane < N_EXPERTS, logits, -jnp.inf)
    m1 = jnp.max(lg, axis=1, keepdims=True)
    i1 = jnp.min(jnp.where(lg == m1, lane, float(LANES)), axis=1, keepdims=True)
    lg2 = jnp.where(lane == i1, -jnp.inf, lg)
    m2 = jnp.max(lg2, axis=1, keepdims=True)
    i2 = jnp.min(jnp.where(lg2 == m2, lane, float(LANES)), axis=1, keepdims=True)
    e2 = jnp.exp(m2 - m1)
    den = 1.0 + e2
    comb_ref[0] = jnp.where(lane == i1, 1.0 / den, 0.0) + jnp.where(lane == i2, e2 / den, 0.0)
    sel = (lane == i1) | (lane == i2)
    earlier = (lax.broadcasted_iota(jnp.int32, (tm, tm), 1) < lax.broadcasted_iota(jnp.int32, (tm, tm), 0))
    sel_f = jnp.where(sel, 1.0, 0.0)
    rank = jnp.dot(jnp.where(earlier, 1.0, 0.0).astype(BF16), sel_f.astype(BF16), preferred_element_type=F32)
    pos = jnp.where(sel, rank, -1.0)
    pos_ref[0] = pos
    post_ref[0] = pos.T[0:SUBLANES, :]
    cnt_ref[0] = jnp.sum(sel_f, axis=0, keepdims=True)


def _router(x, sc, sh, w_router_pad, *, tm):
    b, s, d = x.shape
    nbs = s // tm
    tok = lambda w: pl.BlockSpec((1, tm, w), lambda bi, i: (bi, i, 0))
    return pl.pallas_call(
        _router_kernel,
        grid=(b, nbs),
        in_specs=[
            tok(d),
            pl.BlockSpec((1, 1, d), lambda bi, i: (bi, 0, 0)),
            pl.BlockSpec((1, 1, d), lambda bi, i: (bi, 0, 0)),
            pl.BlockSpec((d, LANES), lambda bi, i: (0, 0)),
        ],
        out_specs=[
            tok(d), tok(LANES), tok(LANES),
            pl.BlockSpec((1, SUBLANES, tm), lambda bi, i: (bi * nbs + i, 0, 0)),
            pl.BlockSpec((1, 1, LANES), lambda bi, i: (bi * nbs + i, 0, 0)),
        ],
        out_shape=[
            jax.ShapeDtypeStruct((b, s, d), BF16),
            jax.ShapeDtypeStruct((b, s, LANES), F32),
            jax.ShapeDtypeStruct((b, s, LANES), F32),
            jax.ShapeDtypeStruct((b * nbs, SUBLANES, tm), F32),
            jax.ShapeDtypeStruct((b * nbs, 1, LANES), F32),
        ],
        compiler_params=_cparams("parallel", "parallel"),
        name="moe_router",
    )(x, sc, sh, w_router_pad)


def _moe_dispatch_kernel(blk_ref, exp_ref, sub_ref, dst_ref, n_ref, h_ref, post_ref, zero_ref, o_ref):
    del zero_ref
    s = pl.program_id(0)

    @pl.when(s < n_ref[0])
    def _():
        rows = o_ref.shape[0]
        pos_row = post_ref[0, pl.ds(exp_ref[s], 1), :]
        want = sub_ref[s] * rows + lax.broadcasted_iota(jnp.int32, (rows, 1), 0)
        onehot = jnp.where(pos_row == want.astype(F32), 1.0, 0.0).astype(BF16)
        o_ref[...] = jnp.dot(onehot, h_ref[...], preferred_element_type=F32).astype(o_ref.dtype)


def _moe_dispatch(lists, h2d, post, zeros, *, bi, rows):
    blk, exp, sub, dst, n = lists
    d = h2d.shape[1]
    grid_spec = pltpu.PrefetchScalarGridSpec(
        num_scalar_prefetch=5,
        grid=(blk.shape[0],),
        in_specs=[
            pl.BlockSpec((bi, d), lambda s, blk, exp, sub, dst, n: (blk[s], 0)),
            pl.BlockSpec((1, SUBLANES, bi), lambda s, blk, exp, sub, dst, n: (blk[s], 0, 0)),
            pl.BlockSpec(memory_space=pl.ANY),
        ],
        out_specs=pl.BlockSpec((rows, d), lambda s, blk, exp, sub, dst, n: (dst[s], 0)),
    )
    return pl.pallas_call(
        _moe_dispatch_kernel,
        grid_spec=grid_spec,
        out_shape=jax.ShapeDtypeStruct(zeros.shape, zeros.dtype),
        input_output_aliases={7: 0},
        compiler_params=_cparams("arbitrary"),
        name="moe_dispatch",
    )(blk, exp, sub, dst, n, h2d, post, zeros)


def _moe_expert_kernel(exp_ref, tile_ref, n_ref, xg_ref, wg_ref, wu_ref, wd_ref, o_ref, *, n_split):
    del exp_ref, tile_ref
    s = pl.program_id(0)

    @pl.when(s < n_ref[0])
    def _():
        xg = xg_ref[...]
        ff = wg_ref.shape[2]
        tf = ff // n_split
        acc = jnp.zeros(o_ref.shape, F32)
        for f in range(n_split):
            fs = slice(f * tf, (f + 1) * tf)
            g = jnp.dot(xg, wg_ref[0, :, fs], preferred_element_type=F32)
            u = jnp.dot(xg, wu_ref[0, :, fs], preferred_element_type=F32)
            acc = acc + jnp.dot((_silu(g) * u).astype(BF16), wd_ref[0, fs, :], preferred_element_type=F32)
        o_ref[...] = acc.astype(o_ref.dtype)


def _moe_experts(exp2, tile2, n2, xg, wg, wu, wd, *, rows2):
    _, d, ff = wg.shape
    grid_spec = pltpu.PrefetchScalarGridSpec(
        num_scalar_prefetch=3,
        grid=(exp2.shape[0],),
        in_specs=[
            pl.BlockSpec((rows2, d), lambda s, exp, tile, n: (tile[s], 0)),
            pl.BlockSpec((1, d, ff), lambda s, exp, tile, n: (exp[s], 0, 0)),
            pl.BlockSpec((1, d, ff), lambda s, exp, tile, n: (exp[s], 0, 0)),
            pl.BlockSpec((1, ff, d), lambda s, exp, tile, n: (exp[s], 0, 0)),
        ],
        out_specs=pl.BlockSpec((rows2, d), lambda s, exp, tile, n: (tile[s], 0)),
    )
    return pl.pallas_call(
        functools.partial(_moe_expert_kernel, n_split=2),
        grid_spec=grid_spec,
        out_shape=jax.ShapeDtypeStruct(xg.shape, BF16),
        compiler_params=_cparams("arbitrary"),
        name="moe_experts",
    )(exp2, tile2, n2, xg, wg, wu, wd)


def _moe_combine_kernel(blk_ref, expa_ref, suba_ref, dsta_ref, expb_ref, subb_ref, dstb_ref, first_ref, last_ref,
                        n_ref, ya_ref, yb_ref, pos_ref, comb_ref, x_ref, gate_ref, lng_ref, lnb_ref, o_ref,
                        acc_ref):
    del blk_ref, dsta_ref, dstb_ref
    s = pl.program_id(0)

    @pl.when(s < n_ref[0])
    def _():
        @pl.when(first_ref[s] == 1)
        def _():
            acc_ref[...] = jnp.zeros(acc_ref.shape, F32)

        rows = ya_ref.shape[0]
        pos = pos_ref[...]
        comb = comb_ref[...]
        lane = lax.broadcasted_iota(jnp.int32, pos.shape, 1)
        row_id = lax.broadcasted_iota(jnp.int32, (1, rows), 1)

        def scatter(e, sub):
            pos_e = jnp.max(jnp.where(lane == e, pos, -2.0), axis=1, keepdims=True)
            w_e = jnp.sum(jnp.where(lane == e, comb, 0.0), axis=1, keepdims=True)
            return jnp.where(pos_e == (sub * rows + row_id).astype(F32), w_e, 0.0).astype(BF16)

        both = jnp.concatenate([scatter(expa_ref[s], suba_ref[s]), scatter(expb_ref[s], subb_ref[s])], axis=1)
        y = jnp.concatenate([ya_ref[...], yb_ref[...]], axis=0)
        acc_ref[...] += jnp.dot(both, y, preferred_element_type=F32)

        @pl.when(last_ref[s] == 1)
        def _():
            r = ALPHA * x_ref[...] + (1.0 + gate_ref[0]) * acc_ref[...]
            o_ref[...] = _layer_norm(r) * lng_ref[...] + lnb_ref[...]


def _moe_combine(pairs, y, pos2d, comb2d, x2d, gate, lng, lnb, *, bi, rows, blocks_per_batch):
    d = x2d.shape[1]
    by_blk = lambda w: pl.BlockSpec((bi, w), lambda s, blk, *_: (blk[s], 0))
    grid_spec = pltpu.PrefetchScalarGridSpec(
        num_scalar_prefetch=10,
        grid=(pairs[0].shape[0],),
        in_specs=[
            pl.BlockSpec((rows, d), lambda s, blk, ea, sa, da, *_: (da[s], 0)),
            pl.BlockSpec((rows, d), lambda s, blk, ea, sa, da, eb, sb, db, *_: (db[s], 0)),
            by_blk(LANES), by_blk(LANES), by_blk(d),
            pl.BlockSpec((1, 1, d), lambda s, blk, *_: (blk[s] // blocks_per_batch, 0, 0)),
            pl.BlockSpec((1, d), lambda s, *_: (0, 0)),
            pl.BlockSpec((1, d), lambda s, *_: (0, 0)),
        ],
        out_specs=by_blk(d),
        scratch_shapes=[pltpu.VMEM((bi, d), F32)],
    )
    return pl.pallas_call(
        _moe_combine_kernel,
        grid_spec=grid_spec,
        out_shape=jax.ShapeDtypeStruct(x2d.shape, F32),
        compiler_params=_cparams("arbitrary"),
        name="moe_combine",
    )(*pairs, y, y, pos2d, comb2d, x2d, gate, lng, lnb)


def _moe_tile_lists(cnt, *, n_max, n2_max, n_pairs_max, rows, group):
    nblk, ne = cnt.shape
    i32 = jnp.int32
    nt = (cnt + rows - 1) // rows
    nt_e_pad = (nt.sum(axis=0) + group - 1) // group * group
    e_end = jnp.cumsum(nt_e_pad)
    dst0 = ((e_end - nt_e_pad)[None, :] + jnp.cumsum(nt, axis=0) - nt).reshape(-1)
    flat = nt.reshape(-1)
    ends = jnp.cumsum(flat)
    n_tiles = ends[-1]
    count_le = lambda bounds, v: jnp.sum(bounds[None, :] <= v[:, None], axis=1).astype(i32)
    slot = jnp.minimum(jnp.arange(n_max, dtype=i32), n_tiles - 1)
    seg = count_le(ends, slot)
    blk = seg // ne
    exp = seg % ne
    sub = slot - (ends[seg] - flat[seg])
    dst = dst0[seg] + sub
    tiles = (blk.astype(i32), exp.astype(i32), sub.astype(i32), dst.astype(i32), n_tiles.astype(i32)[None])
    n2 = e_end[-1] // group
    tile2 = jnp.minimum(jnp.arange(n2_max, dtype=i32), n2 - 1)
    exp2 = jnp.minimum(count_le(e_end, tile2 * group), ne - 1)
    experts = (exp2.astype(i32), tile2.astype(i32), n2.astype(i32)[None])
    nt_blk = nt.sum(axis=1)
    blk_end = ends.reshape(nblk, ne)[:, -1]
    blk_begin = blk_end - nt_blk
    np_blk = (nt_blk + 1) // 2
    p_end = jnp.cumsum(np_blk)
    n_pairs = p_end[-1]
    pslot = jnp.minimum(jnp.arange(n_pairs_max, dtype=i32), n_pairs - 1)
    pblk = count_le(p_end, pslot)
    idx = pslot - (p_end[pblk] - np_blk[pblk])
    ta = blk_begin[pblk] + 2 * idx
    has_b = ta + 1 < blk_end[pblk]
    tb = jnp.where(has_b, ta + 1, ta)
    no_match = jnp.int32(1 << 20)
    pairs = (pblk.astype(i32), exp[ta].astype(i32), sub[ta].astype(i32), dst[ta].astype(i32),
             exp[tb].astype(i32), jnp.where(has_b, sub[tb], no_match).astype(i32), dst[tb].astype(i32),
             (idx == 0).astype(i32), (idx == np_blk[pblk] - 1).astype(i32), n_pairs.astype(i32)[None])
    return tiles, experts, pairs


def _tile(s, pref):
    return min(pref, s)


def _ssd_layer(x, sc, sh, gate, lng, lnb, w_in, conv_w, conv_b, dt_bias, a_log, d_skip, norm_w, w_out):
    s = x.shape[1]
    n_heads = dt_bias.shape[0]
    d_inner = n_heads * SSD_HEAD_DIM
    n_zxbc = w_in.shape[1] - n_heads
    pad = LANES - n_heads
    w_zxbc = w_in[:, :n_zxbc].astype(BF16)
    w_dt = jnp.pad(w_in[:, n_zxbc:], ((0, 0), (0, pad))).astype(BF16)
    zxbc, dt_raw = _ssd_in_proj(x, sc, sh, w_zxbc, w_dt, tm=_tile(s, 1024), tn=d_inner)
    yg = _ssd_scan(
        zxbc, dt_raw, conv_w, conv_b[None, :],
        jnp.pad(dt_bias, (0, pad))[None, :], jnp.pad(a_log, (0, pad))[None, :],
        jnp.repeat(d_skip, SSD_HEAD_DIM)[None, :], norm_w[None, :])
    return _mm_res_ln(yg, w_out.astype(BF16), x, gate, lng, lnb, tm=_tile(s, 512), name="ssd_out_proj")


def _mla_layer(x, positions, sc, sh, gate, lng, lnb, w_in, q_norm, kv_norm, w_uq, w_ukv, w_out):
    s = x.shape[1]
    d = x.shape[2]
    nh, hp = MLA_N_HEADS, MLA_HEAD_PAD
    qk = MLA_NOPE + MLA_ROPE
    rope_lo = MLA_NOPE
    w_cq_ckv = w_in[:, :MLA_Q_RANK + MLA_KV_RANK]
    w_kr = jnp.pad(w_in[:, MLA_Q_RANK + MLA_KV_RANK:], ((0, 0), (rope_lo, hp - qk)))
    w_in_pad = jnp.concatenate([w_cq_ckv, w_kr], axis=1).astype(BF16)
    cin = _mod_matmul(x, sc, sh, w_in_pad, tm=_tile(s, 1024), tn=MLA_IN_PAD, out_dtype=F32, name="mla_in_proj")
    w_uq_pad = jnp.pad(w_uq.reshape(MLA_Q_RANK, nh, qk), ((0, 0), (0, 0), (0, hp - qk)))
    w_uq_pad = w_uq_pad.reshape(MLA_Q_RANK, nh * hp).astype(BF16)
    w_ukv3 = w_ukv.reshape(MLA_KV_RANK, nh, MLA_NOPE + MLA_V)
    w_uk_pad = jnp.pad(w_ukv3[:, :, :MLA_NOPE], ((0, 0), (0, 0), (0, hp - MLA_NOPE)))
    w_uk_pad = w_uk_pad.reshape(MLA_KV_RANK, nh * hp).astype(BF16)
    w_uv = jnp.pad(w_ukv3[:, :, MLA_NOPE:], ((0, 0), (0, 0), (0, hp - MLA_V)))
    w_uv = w_uv.reshape(MLA_KV_RANK, nh * hp).astype(BF16)
    half = MLA_ROPE // 2
    freqs = ROPE_THETA ** (-jnp.arange(half, dtype=F32) / half)
    zeros = lambda n: jnp.zeros((n,), F32)
    freq_row = jnp.concatenate([zeros(rope_lo), freqs, freqs, zeros(hp - qk)])[None, :]
    sign_row = jnp.concatenate([zeros(rope_lo), -jnp.ones((half,), F32), jnp.ones((half,), F32),
                                zeros(hp - qk)])[None, :]
    q, k, v = _mla_proj(cin, positions[:, :, None], freq_row, sign_row, q_norm[None, :], kv_norm[None, :],
                        w_uq_pad, w_uk_pad, w_uv, tm=_tile(s, 512))
    attn = _attention(q, k, v, tq=_tile(s, 512))
    return _mm_res_ln(attn, w_out.astype(BF16), x, gate, lng, lnb, tm=_tile(s, 512), name="mla_out_proj")


def _sg_layer(x, sc, sh, gate, lng, lnb, w_in, b_in, ln_g, ln_b, w_s, b_s, w_out):
    s = x.shape[1]
    uv = _sg_in_proj(x, sc, sh, w_in.astype(BF16), b_in[None, :], ln_g[None, :], ln_b[None, :], tm=_tile(s, 512))
    return _sg_out(uv, w_s, b_s.T, w_out.astype(BF16), x, gate, lng, lnb, tm=_tile(s, 512))


def _dense_ffn(x, sc, sh, gate, lng, lnb, w_gate, w_up, w_down):
    s = x.shape[1]
    ff = w_gate.shape[1]
    return _swiglu_res_ln(x, sc, sh, w_gate.astype(BF16), w_up.astype(BF16), w_down.astype(BF16),
                          gate, lng, lnb, tm=_tile(s, 512), tf=ff // 2)


def _moe_ffn(x, sc, sh, gate, lng, lnb, w_router, w_gate, w_up, w_down):
    b, s, d = x.shape
    ne = w_router.shape[1]
    bi = _tile(s, MOE_BLOCK)
    nblk = b * (s // bi)
    tokens = b * s
    n_max = 2 * tokens // MOE_ROWS + nblk * ne
    n2_max = (n_max + ne * (MOE_GROUP - 1) + MOE_GROUP - 1) // MOE_GROUP
    w_router_pad = jnp.pad(w_router, ((0, 0), (0, LANES - ne)))
    h, comb, pos, post, cnt = _router(x, sc, sh, w_router_pad, tm=bi)
    tiles, experts, pairs = _moe_tile_lists(
        cnt[:, 0, :ne].astype(jnp.int32), n_max=n_max, n2_max=n2_max, n_pairs_max=(n_max + nblk + 1) // 2,
        rows=MOE_ROWS, group=MOE_GROUP)
    zeros = jnp.zeros((n2_max * MOE_GROUP * MOE_ROWS, d), BF16)
    xg = _moe_dispatch(tiles, h.reshape(tokens, d), post, zeros, bi=bi, rows=MOE_ROWS)
    y = _moe_experts(*experts, xg, w_gate.astype(BF16), w_up.astype(BF16), w_down.astype(BF16),
                     rows2=MOE_GROUP * MOE_ROWS)
    out = _moe_combine(pairs, y, pos.reshape(tokens, LANES), comb.reshape(tokens, LANES), x.reshape(tokens, d),
                       gate, lng, lnb, bi=bi, rows=MOE_ROWS, blocks_per_batch=s // bi)
    return out.reshape(b, s, d)


def kernel(x, c, positions, ada_w, ada_b, ln_g, ln_b, ssd_w_in, ssd_conv_w, ssd_conv_b, ssd_dt_bias, ssd_a_log, ssd_d_skip, ssd_norm_w, ssd_w_out, mla_w_in, mla_q_norm, mla_kv_norm, mla_w_uq, mla_w_ukv, mla_w_out, sg_w_in, sg_b_in, sg_ln_g, sg_ln_b, sg_w_s, sg_b_s, sg_w_out, ffn_w_gate, ffn_w_up, ffn_w_down, moe_w_router, moe_w_gate, moe_w_up, moe_w_down):
    batch = x.shape[0]
    depth = ada_w.shape[0]
    c_pad = jnp.pad(c, ((0, SUBLANES - batch), (0, 0)))
    mod = _ada_mod(c_pad, ada_w, ada_b)[:, :, :batch]
    for i in range(depth):
        sh_m, sc_m, g_m, sh_f, sc_f, g_f = [mod[i, t][:, None, :] for t in range(6)]
        lng_m, lnb_m = ln_g[i, 0][None, :], ln_b[i, 0][None, :]
        lng_f, lnb_f = ln_g[i, 1][None, :], ln_b[i, 1][None, :]
        kind, j = i % 3, i // 3
        if kind == 0:
            x = _ssd_layer(x, sc_m, sh_m, g_m, lng_m, lnb_m, ssd_w_in[j], ssd_conv_w[j], ssd_conv_b[j],
                           ssd_dt_bias[j], ssd_a_log[j], ssd_d_skip[j], ssd_norm_w[j], ssd_w_out[j])
        elif kind == 1:
            x = _mla_layer(x, positions, sc_m, sh_m, g_m, lng_m, lnb_m, mla_w_in[j], mla_q_norm[j],
                           mla_kv_norm[j], mla_w_uq[j], mla_w_ukv[j], mla_w_out[j])
        else:
            x = _sg_layer(x, sc_m, sh_m, g_m, lng_m, lnb_m, sg_w_in[j], sg_b_in[j], sg_ln_g[j], sg_ln_b[j],
                          sg_w_s[j], sg_b_s[j], sg_w_out[j])
        k = i // 2
        if i % 2 == 0:
            x = _dense_ffn(x, sc_f, sh_f, g_f, lng_f, lnb_f, ffn_w_gate[k], ffn_w_up[k], ffn_w_down[k])
        else:
            x = _moe_ffn(x, sc_f, sh_f, g_f, lng_f, lnb_f, moe_w_router[k], moe_w_gate[k], moe_w_up[k],
                         moe_w_down[k])
    return x
```

```python
import functools
import math

import jax
import jax.numpy as jnp
from jax import lax
from jax.experimental import pallas as pl
from jax.experimental.pallas import tpu as pltpu

F32 = jnp.float32
BF16 = jnp.bfloat16
HIGHEST = lax.Precision.HIGHEST

DEPTH = 4
ALPHA = (2.0 * DEPTH) ** 0.25
LN_EPS = 1e-5
RMS_EPS = 1e-6
ROPE_THETA = 10000.0

LANES = 128
SUBLANES = 8
VMEM_LIMIT = 56 * 1024 * 1024

SSD_HEAD_DIM = 64
SSD_N_GROUPS = 8
SSD_HPG = 4
SSD_D_STATE = 128
SSD_CONV = 4
SSD_CHUNK = 128
SSD_GROUP_W = SSD_HPG * SSD_HEAD_DIM

MLA_N_HEADS = 16
MLA_NOPE = 64
MLA_ROPE = 32
MLA_V = 64
MLA_Q_RANK = 512
MLA_KV_RANK = 256
MLA_HEAD_PAD = 128
MLA_IN_PAD = MLA_Q_RANK + MLA_KV_RANK + MLA_HEAD_PAD
QK_SCALE = (MLA_NOPE + MLA_ROPE) ** -0.5
LOG2E = 1.4426950408889634
NEG_BIG = -1e30

SG_GROUPS = 8
SG_CHUNK = 128

N_EXPERTS = 8
MOE_BLOCK = 1024
MOE_ROWS = 128
MOE_GROUP = 4
MOE_COMBINE_TILES = 4


def _cparams(*sem):
    return pltpu.CompilerParams(dimension_semantics=sem, vmem_limit_bytes=VMEM_LIMIT)


def _layer_norm(r):
    mu = jnp.mean(r, axis=-1, keepdims=True)
    d = r - mu
    var = jnp.mean(d * d, axis=-1, keepdims=True)
    return d * lax.rsqrt(var + LN_EPS)


def _silu(v):
    return v * jax.nn.sigmoid(v)


def _modulate(x_ref, sc_ref, sh_ref):
    return x_ref[0] * (1.0 + sc_ref[0]) + sh_ref[0]


def _res_ln(x_ref, gate_ref, y, lng_ref, lnb_ref):
    r = ALPHA * x_ref[0] + (1.0 + gate_ref[0]) * y
    return _layer_norm(r) * lng_ref[...] + lnb_ref[...]


def _ada_kernel(c_ref, w_ref, b_ref, o_ref):
    cond = _silu(c_ref[...])
    o_ref[0, 0] = jnp.dot(cond, w_ref[0], preferred_element_type=F32, precision=HIGHEST) + b_ref[0, 0]


def _ada_mod(c_pad, ada_w, ada_b):
    depth, d, _ = ada_w.shape
    rows = c_pad.shape[0]
    return pl.pallas_call(
        _ada_kernel,
        grid=(depth, 6),
        in_specs=[
            pl.BlockSpec((rows, d), lambda i, j: (0, 0)),
            pl.BlockSpec((1, d, d), lambda i, j: (i, 0, j)),
            pl.BlockSpec((1, 1, 1, d), lambda i, j: (i, j, 0, 0)),
        ],
        out_specs=pl.BlockSpec((1, 1, rows, d), lambda i, j: (i, j, 0, 0)),
        out_shape=jax.ShapeDtypeStruct((depth, 6, rows, d), F32),
        compiler_params=_cparams("arbitrary", "arbitrary"),
        name="ada_mod",
    )(c_pad, ada_w, ada_b.reshape(depth, 6, 1, d))


def _modmm_kernel(x_ref, sc_ref, sh_ref, w_ref, o_ref, h_ref):
    @pl.when(pl.program_id(2) == 0)
    def _():
        h_ref[...] = _modulate(x_ref, sc_ref, sh_ref).astype(BF16)

    o_ref[0] = jnp.dot(h_ref[...], w_ref[...], preferred_element_type=F32).astype(o_ref.dtype)


def _mod_matmul(x, sc, sh, w, *, tm, tn, out_dtype, name):
    b, s, d = x.shape
    n = w.shape[1]
    return pl.pallas_call(
        _modmm_kernel,
        grid=(b, s // tm, n // tn),
        in_specs=[
            pl.BlockSpec((1, tm, d), lambda bi, i, j: (bi, i, 0)),
            pl.BlockSpec((1, 1, d), lambda bi, i, j: (bi, 0, 0)),
            pl.BlockSpec((1, 1, d), lambda bi, i, j: (bi, 0, 0)),
            pl.BlockSpec((d, tn), lambda bi, i, j: (0, j)),
        ],
        out_specs=pl.BlockSpec((1, tm, tn), lambda bi, i, j: (bi, i, j)),
        out_shape=jax.ShapeDtypeStruct((b, s, n), out_dtype),
        scratch_shapes=[pltpu.VMEM((tm, d), BF16)],
        compiler_params=_cparams("parallel", "parallel", "arbitrary"),
        name=name,
    )(x, sc, sh, w)


def _ssd_in_kernel(x_ref, sc_ref, sh_ref, w_ref, wdt_ref, o_ref, dt_ref, h_ref):
    @pl.when(pl.program_id(2) == 0)
    def _():
        h_ref[...] = _modulate(x_ref, sc_ref, sh_ref).astype(BF16)
        dt_ref[0] = jnp.dot(h_ref[...], wdt_ref[...], preferred_element_type=F32)

    o_ref[0] = jnp.dot(h_ref[...], w_ref[...], preferred_element_type=F32).astype(o_ref.dtype)


def _ssd_in_proj(x, sc, sh, w_zxbc, w_dt, *, tm, tn):
    b, s, d = x.shape
    n = w_zxbc.shape[1]
    return pl.pallas_call(
        _ssd_in_kernel,
        grid=(b, s // tm, n // tn),
        in_specs=[
            pl.BlockSpec((1, tm, d), lambda bi, i, j: (bi, i, 0)),
            pl.BlockSpec((1, 1, d), lambda bi, i, j: (bi, 0, 0)),
            pl.BlockSpec((1, 1, d), lambda bi, i, j: (bi, 0, 0)),
            pl.BlockSpec((d, tn), lambda bi, i, j: (0, j)),
            pl.BlockSpec((d, LANES), lambda bi, i, j: (0, 0)),
        ],
        out_specs=[
            pl.BlockSpec((1, tm, tn), lambda bi, i, j: (bi, i, j)),
            pl.BlockSpec((1, tm, LANES), lambda bi, i, j: (bi, i, 0)),
        ],
        out_shape=[
            jax.ShapeDtypeStruct((b, s, n), BF16),
            jax.ShapeDtypeStruct((b, s, LANES), F32),
        ],
        scratch_shapes=[pltpu.VMEM((tm, d), BF16)],
        compiler_params=_cparams("parallel", "parallel", "arbitrary"),
        name="ssd_in_proj",
    )(x, sc, sh, w_zxbc, w_dt)


def _ssd_kernel(z_ref, x_ref, bc_ref, dt_ref, cwx_ref, cwbc_ref, cbx_ref, cbbc_ref, dtb_ref, alog_ref,
                dskip_ref, nw_ref, o_ref, prevx_ref, prevbc_ref, taps_ref, xs_ref, b_ref, c_ref, xh_ref,
                acum_ref, key_ref, wend_ref, state_ref):
    L = SSD_CHUNK
    gw = SSD_GROUP_W
    n_st = SSD_D_STATE
    slab = 2 * SUBLANES
    ci = pl.program_id(1)

    @pl.when(ci == 0)
    def _():
        prevx_ref[...] = jnp.zeros(prevx_ref.shape, BF16)
        prevbc_ref[...] = jnp.zeros(prevbc_ref.shape, BF16)
        state_ref[...] = jnp.zeros(state_ref.shape, F32)

    t_idx = lax.broadcasted_iota(jnp.int32, (L, SSD_CONV * L), 0)
    c_idx = lax.broadcasted_iota(jnp.int32, (L, SSD_CONV * L), 1)
    shift = (SSD_CONV - 1) - c_idx // L
    delta = t_idx - c_idx % L
    rot = jnp.where((delta == shift) | (delta == shift - L), 1.0, 0.0).astype(BF16)
    slab_row = lax.broadcasted_iota(jnp.int32, (slab, x_ref.shape[2]), 0)

    def conv(cur_ref, prev_ref, cw_ref, cb_ref):
        body = cur_ref[0, 0:L - slab, :]
        last = cur_ref[0, L - slab:L, :]
        prev = prev_ref[...]
        for k in range(SSD_CONV):
            w_k = cw_ref[k:k + 1, :].astype(BF16)
            swapped = jnp.where(slab_row >= slab - (SSD_CONV - 1 - k), prev, last)
            taps_ref[k * L:(k + 1) * L - slab, :] = body * w_k
            taps_ref[(k + 1) * L - slab:(k + 1) * L, :] = swapped * w_k
        prev_ref[...] = last
        return _silu(jnp.dot(rot, taps_ref[...], preferred_element_type=F32) + cb_ref[...])

    xs = conv(x_ref, prevx_ref, cwx_ref, cbx_ref)
    xs_ref[...] = xs
    xh_ref[...] = xs.astype(BF16)
    bc = conv(bc_ref, prevbc_ref, cwbc_ref, cbbc_ref)
    half = bc.shape[1] // 2
    b_ref[...] = bc[:, :half]
    c_ref[...] = bc[:, half:].astype(BF16)

    dtv = dt_ref[0] + dtb_ref[...]
    dt = jnp.maximum(dtv, 0.0) + jnp.log1p(jnp.exp(-jnp.abs(dtv)))
    a_neg = -jnp.exp(alog_ref[...])
    row = lax.broadcasted_iota(jnp.int32, (L, L), 0)
    col = lax.broadcasted_iota(jnp.int32, (L, L), 1)
    causal = col <= row
    tri = jnp.where(causal, 1.0, 0.0).astype(F32)
    a_cum = jnp.dot(tri, dt * (a_neg * LOG2E), preferred_element_type=F32, precision=HIGHEST)
    a_cum_t = a_cum.T
    key_t = a_cum_t - jnp.log2(dt.T)
    w_end_t = jnp.exp2(a_cum_t[:, L - 1:L] - key_t)

    acum_ref[...] = a_cum
    key_ref[...] = key_t
    wend_ref[...] = w_end_t

    def group(g):
        lane = lax.broadcasted_iota(jnp.int32, (L, LANES), 1)
        head_of_lane = lax.broadcasted_iota(jnp.int32, (L, gw), 1) // SSD_HEAD_DIM
        causal = lax.broadcasted_iota(jnp.int32, (L, L), 1) <= lax.broadcasted_iota(jnp.int32, (L, L), 0)
        sl = slice(g * gw, (g + 1) * gw)
        sn = slice(g * n_st, (g + 1) * n_st)
        c_g = c_ref[:, sn]
        b_g32 = b_ref[:, sn]
        b_gt = b_g32.T
        cb = lax.dot_general(c_g, b_g32.astype(BF16), (((1,), (1,)), ((), ())), preferred_element_type=F32)
        st = state_ref[g]
        y_off = jnp.dot(c_g, st.astype(BF16), preferred_element_type=F32)
        xh_g = xh_ref[:, sl]
        m_heads, bw_heads, x_heads, ea_heads = [], [], [], []
        for r in range(SSD_HPG):
            h = SSD_HPG * g + r
            a_col = jnp.broadcast_to(acum_ref[:, h:h + 1], (L, L))
            decay_dt = jnp.exp2(jnp.where(causal, a_col - key_ref[h:h + 1, :], NEG_BIG))
            m_heads.append((cb * decay_dt).astype(BF16))
            bw_heads.append((b_gt * wend_ref[h:h + 1, :]).astype(BF16))
            x_heads.append(jnp.where(head_of_lane == r, xh_g, jnp.zeros_like(xh_g)))
            ea_heads.append(jnp.exp2(a_col))
        x_stack = jnp.concatenate(x_heads, axis=0)
        y_diag = jnp.dot(jnp.concatenate(m_heads, axis=1), x_stack, preferred_element_type=F32)
        new_st = jnp.dot(jnp.concatenate(bw_heads, axis=1), x_stack, preferred_element_type=F32)
        ea_g = jnp.concatenate([jnp.where(lane < SSD_HEAD_DIM, ea_heads[0], ea_heads[1]),
                                jnp.where(lane < SSD_HEAD_DIM, ea_heads[2], ea_heads[3])], axis=1)
        y = y_diag + y_off * ea_g + xs_ref[:, sl] * dskip_ref[:, sl]
        z_g = z_ref[0, :, sl].astype(F32)
        t = y * _silu(z_g)
        ms = jnp.mean(t * t, axis=-1, keepdims=True)
        o_ref[0, :, sl] = (t * lax.rsqrt(ms + RMS_EPS) * nw_ref[:, sl]).astype(o_ref.dtype)
        state_ref[g] = st * ea_g[L - 1:L, :] + new_st

    for g in range(SSD_N_GROUPS):
        group(g)


def _ssd_scan(zxbc, dt_raw, conv_w, conv_b, dt_bias, a_log, d_skip, norm_w):
    b, s, _ = zxbc.shape
    L = SSD_CHUNK
    di = SSD_N_GROUPS * SSD_GROUP_W
    blk = lambda j: pl.BlockSpec((1, L, di), lambda bi, ci, j=j: (bi, ci, j))
    vec = lambda w: pl.BlockSpec((1, w), lambda bi, ci: (0, 0))
    cw = lambda j: pl.BlockSpec((SSD_CONV, di), lambda bi, ci, j=j: (0, j))
    cb = lambda j: pl.BlockSpec((1, di), lambda bi, ci, j=j: (0, j))
    return pl.pallas_call(
        _ssd_kernel,
        grid=(b, s // L),
        in_specs=[
            blk(0), blk(1), blk(2),
            pl.BlockSpec((1, L, LANES), lambda bi, ci: (bi, ci, 0)),
            cw(0), cw(1), cb(0), cb(1),
            vec(LANES), vec(LANES), vec(di), vec(di),
        ],
        out_specs=pl.BlockSpec((1, L, di), lambda bi, ci: (bi, ci, 0)),
        out_shape=jax.ShapeDtypeStruct((b, s, di), BF16),
        scratch_shapes=[
            pltpu.VMEM((2 * SUBLANES, di), BF16),
            pltpu.VMEM((2 * SUBLANES, di), BF16),
            pltpu.VMEM((SSD_CONV * L, di), BF16),
            pltpu.VMEM((L, di), F32),
            pltpu.VMEM((L, di // 2), F32),
            pltpu.VMEM((L, di // 2), BF16),
            pltpu.VMEM((L, di), BF16),
            pltpu.VMEM((L, LANES), F32),
            pltpu.VMEM((LANES, L), F32),
            pltpu.VMEM((LANES, L), F32),
            pltpu.VMEM((SSD_N_GROUPS, SSD_D_STATE, SSD_GROUP_W), F32),
        ],
        compiler_params=_cparams("arbitrary", "arbitrary"),
        name="ssd_scan",
    )(zxbc, zxbc, zxbc, dt_raw, conv_w, conv_w, conv_b, conv_b, dt_bias, a_log, d_skip, norm_w)


def _mm_res_ln_kernel(a_ref, w_ref, x_ref, gate_ref, lng_ref, lnb_ref, o_ref):
    y = jnp.dot(a_ref[0], w_ref[...], preferred_element_type=F32)
    o_ref[0] = _res_ln(x_ref, gate_ref, y, lng_ref, lnb_ref)


def _mm_res_ln(a, w, x, gate, lng, lnb, *, tm, name):
    b, s, d = x.shape
    k = a.shape[2]
    return pl.pallas_call(
        _mm_res_ln_kernel,
        grid=(b, s // tm),
        in_specs=[
            pl.BlockSpec((1, tm, k), lambda bi, i: (bi, i, 0)),
            pl.BlockSpec((k, d), lambda bi, i: (0, 0)),
            pl.BlockSpec((1, tm, d), lambda bi, i: (bi, i, 0)),
            pl.BlockSpec((1, 1, d), lambda bi, i: (bi, 0, 0)),
            pl.BlockSpec((1, d), lambda bi, i: (0, 0)),
            pl.BlockSpec((1, d), lambda bi, i: (0, 0)),
        ],
        out_specs=pl.BlockSpec((1, tm, d), lambda bi, i: (bi, i, 0)),
        out_shape=jax.ShapeDtypeStruct((b, s, d), F32),
        compiler_params=_cparams("parallel", "parallel"),
        name=name,
    )(a, w, x, gate, lng, lnb)


def _mla_proj_kernel(cin_ref, pos_ref, freq_ref, sign_ref, qn_ref, kvn_ref, wuq_ref, wuk_ref, wuv_ref,
                     q_ref, k_ref, v_ref):
    cin = cin_ref[0]
    tm = cin.shape[0]
    hp = MLA_HEAD_PAD

    def rms(v, w_ref):
        return v * lax.rsqrt(jnp.mean(v * v, axis=-1, keepdims=True) + RMS_EPS) * w_ref[...]

    cq = rms(cin[:, :MLA_Q_RANK], qn_ref).astype(BF16)
    ckv = rms(cin[:, MLA_Q_RANK:MLA_Q_RANK + MLA_KV_RANK], kvn_ref).astype(BF16)
    k_rope = cin[:, MLA_Q_RANK + MLA_KV_RANK:]
    q = jnp.dot(cq, wuq_ref[...], preferred_element_type=F32)
    k = jnp.dot(ckv, wuk_ref[...], preferred_element_type=F32)
    ones_col = jnp.where(lax.broadcasted_iota(jnp.int32, (1, wuv_ref.shape[1]), 1) % hp == MLA_V, 1.0, 0.0)
    v_ref[0] = (jnp.dot(ckv, wuv_ref[...], preferred_element_type=F32) + ones_col).astype(v_ref.dtype)

    ang = pos_ref[0].astype(F32) * freq_ref[...]
    cos = jnp.cos(ang)
    sin = jnp.sin(ang) * sign_ref[...]
    lane = lax.broadcasted_iota(jnp.int32, (tm, hp), 1)
    first_half = lane < MLA_NOPE + MLA_ROPE // 2

    def rope(xh):
        swapped = jnp.where(first_half,
                            pltpu.roll(xh, hp - MLA_ROPE // 2, 1),
                            pltpu.roll(xh, MLA_ROPE // 2, 1))
        return xh * cos + swapped * sin

    kr = rope(k_rope)
    for h in range(MLA_N_HEADS):
        sl = slice(h * hp, (h + 1) * hp)
        q_ref[0, :, sl] = (rope(q[:, sl]) * (QK_SCALE * LOG2E)).astype(q_ref.dtype)
        k_ref[0, :, sl] = (k[:, sl] + kr).astype(k_ref.dtype)


def _mla_proj(cin, pos, freq, sign, q_norm, kv_norm, w_uq, w_uk, w_uv, *, tm):
    b, s, n_in = cin.shape
    nq = w_uq.shape[1]
    nv = w_uv.shape[1]
    full = lambda a: pl.BlockSpec(a.shape, lambda bi, i: (0,) * a.ndim)
    return pl.pallas_call(
        _mla_proj_kernel,
        grid=(b, s // tm),
        in_specs=[
            pl.BlockSpec((1, tm, n_in), lambda bi, i: (bi, i, 0)),
            pl.BlockSpec((1, tm, 1), lambda bi, i: (bi, i, 0)),
            full(freq), full(sign), full(q_norm), full(kv_norm), full(w_uq), full(w_uk), full(w_uv),
        ],
        out_specs=[
            pl.BlockSpec((1, tm, nq), lambda bi, i: (bi, i, 0)),
            pl.BlockSpec((1, tm, nq), lambda bi, i: (bi, i, 0)),
            pl.BlockSpec((1, tm, nv), lambda bi, i: (bi, i, 0)),
        ],
        out_shape=[
            jax.ShapeDtypeStruct((b, s, nq), BF16),
            jax.ShapeDtypeStruct((b, s, nq), BF16),
            jax.ShapeDtypeStruct((b, s, nv), BF16),
        ],
        compiler_params=_cparams("parallel", "parallel"),
        name="mla_proj",
    )(cin, pos, freq, sign, q_norm, kv_norm, w_uq, w_uk, w_uv)


def _attn_kernel(q_ref, k_ref, v_ref, o_ref, acc_ref, s0_ref, s1_ref, *, tq, n_kb):
    qi = pl.program_id(2)
    hp = MLA_HEAD_PAD
    key_minus_query = (lax.broadcasted_iota(jnp.int32, (tq, tq), 1)
                       - lax.broadcasted_iota(jnp.int32, (tq, tq), 0))

    def scores(kb, s_ref):
        start = pl.multiple_of(jnp.minimum(kb, n_kb - 1) * tq, tq)
        keep = key_minus_query <= (qi - kb) * tq
        for hh in range(2):
            q = q_ref[0, :, hh * hp:(hh + 1) * hp]
            k = k_ref[0, pl.ds(start, tq), hh * hp:(hh + 1) * hp]
            s = lax.dot_general(q, k, (((1,), (1,)), ((), ())), preferred_element_type=F32)
            s_ref[hh] = jnp.where(keep, s, NEG_BIG)

    def softmax_pv(kb, s_ref, m_prev):
        start = pl.multiple_of(jnp.minimum(kb, n_kb - 1) * tq, tq)
        m_out = []
        for hh in range(2):
            v = v_ref[0, pl.ds(start, tq), hh * hp:(hh + 1) * hp]
            s = s_ref[hh]
            m_new = jnp.maximum(m_prev[hh], jnp.max(s, axis=1, keepdims=True))
            alpha = jnp.exp2(m_prev[hh] - m_new)
            p = jnp.exp2((s - m_new).astype(BF16))
            acc_ref[hh] = acc_ref[hh] * alpha + jnp.dot(p, v, preferred_element_type=F32)
            m_out.append(m_new)
        return tuple(m_out)

    acc_ref[...] = jnp.zeros(acc_ref.shape, F32)
    m_init = jnp.full((tq, 1), NEG_BIG, F32)
    scores(0, s0_ref)

    def two_blocks(t, m):
        m = softmax_pv(2 * t, s0_ref, m)
        scores(2 * t + 1, s1_ref)
        m = softmax_pv(2 * t + 1, s1_ref, m)
        scores(2 * t + 2, s0_ref)
        return m

    n_pairs = (qi + 2) // 2
    m = lax.fori_loop(0, n_pairs - 1, two_blocks, (m_init, m_init))
    m = softmax_pv(2 * n_pairs - 2, s0_ref, m)
    scores(2 * n_pairs - 1, s1_ref)
    softmax_pv(2 * n_pairs - 1, s1_ref, m)
    out_a = acc_ref[0] / acc_ref[0][:, MLA_V:MLA_V + 1]
    out_b = acc_ref[1] / acc_ref[1][:, MLA_V:MLA_V + 1]
    lane = lax.broadcasted_iota(jnp.int32, (tq, hp), 1)
    o_ref[0] = jnp.where(lane < MLA_V, out_a, pltpu.roll(out_b, MLA_V, 1)).astype(o_ref.dtype)


def _attention(q, k, v, *, tq):
    b, s, _ = q.shape
    pairs = MLA_N_HEADS // 2
    return pl.pallas_call(
        functools.partial(_attn_kernel, tq=tq, n_kb=s // tq),
        grid=(b, pairs, s // tq),
        in_specs=[
            pl.BlockSpec((1, tq, 2 * MLA_HEAD_PAD), lambda bi, j, i: (bi, i, j)),
            pl.BlockSpec((1, s, 2 * MLA_HEAD_PAD), lambda bi, j, i: (bi, 0, j)),
            pl.BlockSpec((1, s, 2 * MLA_HEAD_PAD), lambda bi, j, i: (bi, 0, j)),
        ],
        out_specs=pl.BlockSpec((1, tq, 2 * MLA_V), lambda bi, j, i: (bi, i, j)),
        out_shape=jax.ShapeDtypeStruct((b, s, MLA_N_HEADS * MLA_V), BF16),
        scratch_shapes=[pltpu.VMEM((2, tq, MLA_HEAD_PAD), F32), pltpu.VMEM((2, tq, tq), F32),
                        pltpu.VMEM((2, tq, tq), F32)],
        compiler_params=_cparams("parallel", "parallel", "arbitrary"),
        name="mla_attention",
    )(q, k, v)


def _sg_in_kernel(x_ref, sc_ref, sh_ref, w_ref, b_ref, lng_ref, lnb_ref, o_ref, h_ref):
    j = pl.program_id(2)

    @pl.when(j == 0)
    def _():
        h_ref[...] = _modulate(x_ref, sc_ref, sh_ref).astype(BF16)

    pre = jnp.dot(h_ref[...], w_ref[...], preferred_element_type=F32) + b_ref[...]
    y = jax.nn.gelu(pre.astype(BF16))

    @pl.when(j == 0)
    def _():
        o_ref[0] = y.astype(o_ref.dtype)

    @pl.when(j == 1)
    def _():
        o_ref[0] = (_layer_norm(y.astype(F32)) * lng_ref[...] + lnb_ref[...]).astype(o_ref.dtype)


def _sg_in_proj(x, sc, sh, w, bias, lng, lnb, *, tm):
    b, s, d = x.shape
    n = w.shape[1]
    tn = n // 2
    return pl.pallas_call(
        _sg_in_kernel,
        grid=(b, s // tm, 2),
        in_specs=[
            pl.BlockSpec((1, tm, d), lambda bi, i, j: (bi, i, 0)),
            pl.BlockSpec((1, 1, d), lambda bi, i, j: (bi, 0, 0)),
            pl.BlockSpec((1, 1, d), lambda bi, i, j: (bi, 0, 0)),
            pl.BlockSpec((d, tn), lambda bi, i, j: (0, j)),
            pl.BlockSpec((1, tn), lambda bi, i, j: (0, j)),
            pl.BlockSpec((1, tn), lambda bi, i, j: (0, 0)),
            pl.BlockSpec((1, tn), lambda bi, i, j: (0, 0)),
        ],
        out_specs=pl.BlockSpec((1, tm, tn), lambda bi, i, j: (bi, i, j)),
        out_shape=jax.ShapeDtypeStruct((b, s, n), BF16),
        scratch_shapes=[pltpu.VMEM((tm, d), BF16)],
        compiler_params=_cparams("parallel", "parallel", "arbitrary"),
        name="sg_in_proj",
    )(x, sc, sh, w, bias, lng, lnb)


def _sg_out_kernel(u_ref, v_ref, ws_ref, bst_ref, w_ref, x_ref, gate_ref, lng_ref, lnb_ref, o_ref, gated_ref):
    tm = u_ref.shape[1]
    L = SG_CHUNK
    gd = u_ref.shape[2] // SG_GROUPS
    row = lax.broadcasted_iota(jnp.int32, (L, L), 0)
    col = lax.broadcasted_iota(jnp.int32, (L, L), 1)
    causal = col <= row
    for g in range(SG_GROUPS):
        ws = jnp.where(causal, ws_ref[g], 0.0).astype(BF16)
        bias = bst_ref[:, g:g + 1]
        for c in range(tm // L):
            rs = slice(c * L, (c + 1) * L)
            cs = slice(g * gd, (g + 1) * gd)
            mixed = jnp.dot(ws, v_ref[0, rs, cs], preferred_element_type=F32) + bias
            gated_ref[rs, cs] = (u_ref[0, rs, cs].astype(F32) * mixed).astype(BF16)
    y = jnp.dot(gated_ref[...], w_ref[...], preferred_element_type=F32)
    o_ref[0] = _res_ln(x_ref, gate_ref, y, lng_ref, lnb_ref)


def _sg_out(uv, w_s, b_s_t, w_out, x, gate, lng, lnb, *, tm):
    b, s, d = x.shape
    sgd = uv.shape[2] // 2
    full = lambda a: pl.BlockSpec(a.shape, lambda bi, i: (0,) * a.ndim)
    return pl.pallas_call(
        _sg_out_kernel,
        grid=(b, s // tm),
        in_specs=[
            pl.BlockSpec((1, tm, sgd), lambda bi, i: (bi, i, 0)),
            pl.BlockSpec((1, tm, sgd), lambda bi, i: (bi, i, 1)),
            full(w_s), full(b_s_t), full(w_out),
            pl.BlockSpec((1, tm, d), lambda bi, i: (bi, i, 0)),
            pl.BlockSpec((1, 1, d), lambda bi, i: (bi, 0, 0)),
            pl.BlockSpec((1, d), lambda bi, i: (0, 0)),
            pl.BlockSpec((1, d), lambda bi, i: (0, 0)),
        ],
        out_specs=pl.BlockSpec((1, tm, d), lambda bi, i: (bi, i, 0)),
        out_shape=jax.ShapeDtypeStruct((b, s, d), F32),
        scratch_shapes=[pltpu.VMEM((tm, sgd), BF16)],
        compiler_params=_cparams("parallel", "parallel"),
        name="sg_out",
    )(uv, uv, w_s, b_s_t, w_out, x, gate, lng, lnb)


def _swiglu_kernel(x_ref, sc_ref, sh_ref, wg_ref, wu_ref, wd_ref, gate_ref, lng_ref, lnb_ref, o_ref, *, n_split):
    h = _modulate(x_ref, sc_ref, sh_ref).astype(BF16)
    tf = wg_ref.shape[1] // n_split
    acc = jnp.zeros(o_ref.shape[1:], F32)
    for f in range(n_split):
        fs = slice(f * tf, (f + 1) * tf)
        g = jnp.dot(h, wg_ref[:, fs], preferred_element_type=F32)
        u = jnp.dot(h, wu_ref[:, fs], preferred_element_type=F32)
        acc = acc + jnp.dot((_silu(g) * u).astype(BF16), wd_ref[fs, :], preferred_element_type=F32)
    o_ref[0] = _res_ln(x_ref, gate_ref, acc, lng_ref, lnb_ref)


def _swiglu_res_ln(x, sc, sh, wg, wu, wd, gate, lng, lnb, *, tm):
    b, s, d = x.shape
    whole = lambda a: pl.BlockSpec(a.shape, lambda bi, i: (0, 0))
    return pl.pallas_call(
        functools.partial(_swiglu_kernel, n_split=2),
        grid=(b, s // tm),
        in_specs=[
            pl.BlockSpec((1, tm, d), lambda bi, i: (bi, i, 0)),
            pl.BlockSpec((1, 1, d), lambda bi, i: (bi, 0, 0)),
            pl.BlockSpec((1, 1, d), lambda bi, i: (bi, 0, 0)),
            whole(wg), whole(wu), whole(wd),
            pl.BlockSpec((1, 1, d), lambda bi, i: (bi, 0, 0)),
            pl.BlockSpec((1, d), lambda bi, i: (0, 0)),
            pl.BlockSpec((1, d), lambda bi, i: (0, 0)),
        ],
        out_specs=pl.BlockSpec((1, tm, d), lambda bi, i: (bi, i, 0)),
        out_shape=jax.ShapeDtypeStruct((b, s, d), F32),
        compiler_params=_cparams("parallel", "parallel"),
        name="swiglu_res_ln",
    )(x, sc, sh, wg, wu, wd, gate, lng, lnb)


def _router_kernel(x_ref, sc_ref, sh_ref, wr_ref, h_ref, comb_ref, pos_ref, post_ref, cnt_ref):
    h = _modulate(x_ref, sc_ref, sh_ref)
    h_ref[0] = h.astype(h_ref.dtype)
    logits = jnp.dot(h, wr_ref[...], preferred_element_type=F32, precision=HIGHEST)
    tm = logits.shape[0]
    lane = lax.broadcasted_iota(jnp.int32, logits.shape, 1).astype(F32)
    lg = jnp.where(lane < N_EXPERTS, logits, -jnp.inf)
    m1 = jnp.max(lg, axis=1, keepdims=True)
    i1 = jnp.min(jnp.where(lg == m1, lane, float(LANES)), axis=1, keepdims=True)
    lg2 = jnp.where(lane == i1, -jnp.inf, lg)
    m2 = jnp.max(lg2, axis=1, keepdims=True)
    i2 = jnp.min(jnp.where(lg2 == m2, lane, float(LANES)), axis=1, keepdims=True)
    e2 = jnp.exp(m2 - m1)
    den = 1.0 + e2
    comb_ref[0] = jnp.where(lane == i1, 1.0 / den, 0.0) + jnp.where(lane == i2, e2 / den, 0.0)
    sel = (lane == i1) | (lane == i2)
    earlier = (lax.broadcasted_iota(jnp.int32, (tm, tm), 1) < lax.broadcasted_iota(jnp.int32, (tm, tm), 0))
    sel_f = jnp.where(sel, 1.0, 0.0)
    rank = jnp.dot(jnp.where(earlier, 1.0, 0.0).astype(BF16), sel_f.astype(BF16), preferred_element_type=F32)
    pos = jnp.where(sel, rank, -1.0)
    pos_ref[0] = pos
    post_ref[0] = pos.T[0:SUBLANES, :]
    cnt_ref[0] = jnp.sum(sel_f, axis=0, keepdims=True)


def _router(x, sc, sh, w_router_pad, *, tm):
    b, s, d = x.shape
    nbs = s // tm
    tok = lambda w: pl.BlockSpec((1, tm, w), lambda bi, i: (bi, i, 0))
    return pl.pallas_call(
        _router_kernel,
        grid=(b, nbs),
        in_specs=[
            tok(d),
            pl.BlockSpec((1, 1, d), lambda bi, i: (bi, 0, 0)),
            pl.BlockSpec((1, 1, d), lambda bi, i: (bi, 0, 0)),
            pl.BlockSpec((d, LANES), lambda bi, i: (0, 0)),
        ],
        out_specs=[
            tok(d), tok(LANES), tok(LANES),
            pl.BlockSpec((1, SUBLANES, tm), lambda bi, i: (bi * nbs + i, 0, 0)),
            pl.BlockSpec((1, 1, LANES), lambda bi, i: (bi * nbs + i, 0, 0)),
        ],
        out_shape=[
            jax.ShapeDtypeStruct((b, s, d), BF16),
            jax.ShapeDtypeStruct((b, s, LANES), F32),
            jax.ShapeDtypeStruct((b, s, LANES), F32),
            jax.ShapeDtypeStruct((b * nbs, SUBLANES, tm), F32),
            jax.ShapeDtypeStruct((b * nbs, 1, LANES), F32),
        ],
        compiler_params=_cparams("parallel", "parallel"),
        name="moe_router",
    )(x, sc, sh, w_router_pad)


def _moe_dispatch_kernel(blk_ref, exp_ref, sub_ref, dst_ref, n_ref, h_ref, post_ref, zero_ref, o_ref):
    del zero_ref
    s = pl.program_id(0)

    @pl.when(s < n_ref[0])
    def _():
        rows = o_ref.shape[0]
        pos_row = post_ref[0, pl.ds(exp_ref[s], 1), :]
        want = sub_ref[s] * rows + lax.broadcasted_iota(jnp.int32, (rows, 1), 0)
        onehot = jnp.where(pos_row == want.astype(F32), 1.0, 0.0).astype(BF16)
        o_ref[...] = jnp.dot(onehot, h_ref[...], preferred_element_type=F32).astype(o_ref.dtype)


def _moe_dispatch(lists, h2d, post, zeros, *, bi, rows):
    blk, exp, sub, dst, n = lists
    d = h2d.shape[1]
    grid_spec = pltpu.PrefetchScalarGridSpec(
        num_scalar_prefetch=5,
        grid=(blk.shape[0],),
        in_specs=[
            pl.BlockSpec((bi, d), lambda s, blk, exp, sub, dst, n: (blk[s], 0)),
            pl.BlockSpec((1, SUBLANES, bi), lambda s, blk, exp, sub, dst, n: (blk[s], 0, 0)),
            pl.BlockSpec(memory_space=pl.ANY),
        ],
        out_specs=pl.BlockSpec((rows, d), lambda s, blk, exp, sub, dst, n: (dst[s], 0)),
    )
    return pl.pallas_call(
        _moe_dispatch_kernel,
        grid_spec=grid_spec,
        out_shape=jax.ShapeDtypeStruct(zeros.shape, zeros.dtype),
        input_output_aliases={7: 0},
        compiler_params=_cparams("arbitrary"),
        name="moe_dispatch",
    )(blk, exp, sub, dst, n, h2d, post, zeros)


def _moe_expert_kernel(exp_ref, tile_ref, n_ref, xg_ref, wg_ref, wu_ref, wd_ref, o_ref, *, n_split):
    del exp_ref, tile_ref
    s = pl.program_id(0)

    @pl.when(s < n_ref[0])
    def _():
        xg = xg_ref[...]
        ff = wg_ref.shape[2]
        tf = ff // n_split
        acc = jnp.zeros(o_ref.shape, F32)
        for f in range(n_split):
            fs = slice(f * tf, (f + 1) * tf)
            g = jnp.dot(xg, wg_ref[0, :, fs], preferred_element_type=F32)
            u = jnp.dot(xg, wu_ref[0, :, fs], preferred_element_type=F32)
            acc = acc + jnp.dot((_silu(g) * u).astype(BF16), wd_ref[0, fs, :], preferred_element_type=F32)
        o_ref[...] = acc.astype(o_ref.dtype)


def _moe_experts(exp2, tile2, n2, xg, wg, wu, wd, *, rows2):
    _, d, ff = wg.shape
    grid_spec = pltpu.PrefetchScalarGridSpec(
        num_scalar_prefetch=3,
        grid=(exp2.shape[0],),
        in_specs=[
            pl.BlockSpec((rows2, d), lambda s, exp, tile, n: (tile[s], 0)),
            pl.BlockSpec((1, d, ff), lambda s, exp, tile, n: (exp[s], 0, 0)),
            pl.BlockSpec((1, d, ff), lambda s, exp, tile, n: (exp[s], 0, 0)),
            pl.BlockSpec((1, ff, d), lambda s, exp, tile, n: (exp[s], 0, 0)),
        ],
        out_specs=pl.BlockSpec((rows2, d), lambda s, exp, tile, n: (tile[s], 0)),
    )
    return pl.pallas_call(
        functools.partial(_moe_expert_kernel, n_split=2),
        grid_spec=grid_spec,
        out_shape=jax.ShapeDtypeStruct(xg.shape, BF16),
        compiler_params=_cparams("arbitrary"),
        name="moe_experts",
    )(exp2, tile2, n2, xg, wg, wu, wd)


def _moe_combine_kernel(*refs):
    c = MOE_COMBINE_TILES
    first_ref, last_ref, n_ref = refs[1:4]
    slot_refs = refs[4:4 + 3 * c]
    y_refs = refs[4 + 3 * c:4 + 4 * c]
    pos_ref, comb_ref, x_ref, gate_ref, lng_ref, lnb_ref, o_ref, acc_ref = refs[4 + 4 * c:]
    s = pl.program_id(0)

    @pl.when(s < n_ref[0])
    def _():
        @pl.when(first_ref[s] == 1)
        def _():
            acc_ref[...] = jnp.zeros(acc_ref.shape, F32)

        rows = y_refs[0].shape[0]
        pos = pos_ref[...]
        comb = comb_ref[...]
        lane = lax.broadcasted_iota(jnp.int32, pos.shape, 1)
        row_id = lax.broadcasted_iota(jnp.int32, (1, rows), 1)

        def scatter(e, sub):
            pos_e = jnp.max(jnp.where(lane == e, pos, -2.0), axis=1, keepdims=True)
            w_e = jnp.sum(jnp.where(lane == e, comb, 0.0), axis=1, keepdims=True)
            return jnp.where(pos_e == (sub * rows + row_id).astype(F32), w_e, 0.0).astype(BF16)

        scat = jnp.concatenate([scatter(slot_refs[3 * i][s], slot_refs[3 * i + 1][s]) for i in range(c)], axis=1)
        y = jnp.concatenate([r[...] for r in y_refs], axis=0)
        acc_ref[...] += jnp.dot(scat, y, preferred_element_type=F32)

        @pl.when(last_ref[s] == 1)
        def _():
            r = ALPHA * x_ref[...] + (1.0 + gate_ref[0]) * acc_ref[...]
            o_ref[...] = _layer_norm(r) * lng_ref[...] + lnb_ref[...]


def _moe_combine(bundles, y, pos2d, comb2d, x2d, gate, lng, lnb, *, bi, rows, blocks_per_batch):
    d = x2d.shape[1]
    c = MOE_COMBINE_TILES
    by_blk = lambda w: pl.BlockSpec((bi, w), lambda s, blk, *_: (blk[s], 0))
    y_spec = lambda i: pl.BlockSpec((rows, d), lambda s, *pf: (pf[4 + 3 * i + 2][s], 0))
    grid_spec = pltpu.PrefetchScalarGridSpec(
        num_scalar_prefetch=len(bundles),
        grid=(bundles[0].shape[0],),
        in_specs=[y_spec(i) for i in range(c)] + [
            by_blk(LANES), by_blk(LANES), by_blk(d),
            pl.BlockSpec((1, 1, d), lambda s, blk, *_: (blk[s] // blocks_per_batch, 0, 0)),
            pl.BlockSpec((1, d), lambda s, *_: (0, 0)),
            pl.BlockSpec((1, d), lambda s, *_: (0, 0)),
        ],
        out_specs=by_blk(d),
        scratch_shapes=[pltpu.VMEM((bi, d), F32)],
    )
    return pl.pallas_call(
        _moe_combine_kernel,
        grid_spec=grid_spec,
        out_shape=jax.ShapeDtypeStruct(x2d.shape, F32),
        compiler_params=_cparams("arbitrary"),
        name="moe_combine",
    )(*bundles, *([y] * c), pos2d, comb2d, x2d, gate, lng, lnb)


def _moe_tile_lists(cnt, *, n_max, n2_max, n_bundles_max, rows, group):
    nblk, ne = cnt.shape
    i32 = jnp.int32
    nt = (cnt + rows - 1) // rows
    nt_e_pad = (nt.sum(axis=0) + group - 1) // group * group
    e_end = jnp.cumsum(nt_e_pad)
    dst0 = ((e_end - nt_e_pad)[None, :] + jnp.cumsum(nt, axis=0) - nt).reshape(-1)
    flat = nt.reshape(-1)
    ends = jnp.cumsum(flat)
    n_tiles = ends[-1]
    count_le = lambda bounds, v: jnp.sum(bounds[None, :] <= v[:, None], axis=1).astype(i32)
    slot = jnp.minimum(jnp.arange(n_max, dtype=i32), n_tiles - 1)
    seg = count_le(ends, slot)
    blk = seg // ne
    exp = seg % ne
    sub = slot - (ends[seg] - flat[seg])
    dst = dst0[seg] + sub
    tiles = (blk.astype(i32), exp.astype(i32), sub.astype(i32), dst.astype(i32), n_tiles.astype(i32)[None])
    n2 = e_end[-1] // group
    tile2 = jnp.minimum(jnp.arange(n2_max, dtype=i32), n2 - 1)
    exp2 = jnp.minimum(count_le(e_end, tile2 * group), ne - 1)
    experts = (exp2.astype(i32), tile2.astype(i32), n2.astype(i32)[None])
    c = MOE_COMBINE_TILES
    nt_blk = nt.sum(axis=1)
    blk_end = ends.reshape(nblk, ne)[:, -1]
    blk_begin = blk_end - nt_blk
    nb_blk = (nt_blk + c - 1) // c
    b_end = jnp.cumsum(nb_blk)
    n_bundles = b_end[-1]
    bslot = jnp.minimum(jnp.arange(n_bundles_max, dtype=i32), n_bundles - 1)
    bblk = count_le(b_end, bslot)
    idx = bslot - (b_end[bblk] - nb_blk[bblk])
    t0 = blk_begin[bblk] + c * idx
    no_match = jnp.int32(1 << 20)
    bundles = [bblk.astype(i32), (idx == 0).astype(i32), (idx == nb_blk[bblk] - 1).astype(i32),
               n_bundles.astype(i32)[None]]
    for i in range(c):
        real = t0 + i < blk_end[bblk]
        t = jnp.where(real, t0 + i, t0)
        bundles += [exp[t].astype(i32), jnp.where(real, sub[t], no_match).astype(i32), dst[t].astype(i32)]
    return tiles, experts, tuple(bundles)


def _tile(s, pref):
    return min(pref, s)


def _ssd_layer(x, sc, sh, gate, lng, lnb, w_in, conv_w, conv_b, dt_bias, a_log, d_skip, norm_w, w_out):
    s = x.shape[1]
    n_heads = dt_bias.shape[0]
    d_inner = n_heads * SSD_HEAD_DIM
    n_zxbc = w_in.shape[1] - n_heads
    pad = LANES - n_heads
    w_zxbc = w_in[:, :n_zxbc].astype(BF16)
    w_dt = jnp.pad(w_in[:, n_zxbc:], ((0, 0), (0, pad))).astype(BF16)
    zxbc, dt_raw = _ssd_in_proj(x, sc, sh, w_zxbc, w_dt, tm=_tile(s, 1024), tn=d_inner)
    yg = _ssd_scan(
        zxbc, dt_raw, conv_w, conv_b[None, :],
        jnp.pad(dt_bias, (0, pad))[None, :], jnp.pad(a_log, (0, pad))[None, :],
        jnp.repeat(d_skip, SSD_HEAD_DIM)[None, :], norm_w[None, :])
    return _mm_res_ln(yg, w_out.astype(BF16), x, gate, lng, lnb, tm=_tile(s, 512), name="ssd_out_proj")


def _mla_layer(x, positions, sc, sh, gate, lng, lnb, w_in, q_norm, kv_norm, w_uq, w_ukv, w_out):
    s = x.shape[1]
    d = x.shape[2]
    nh, hp = MLA_N_HEADS, MLA_HEAD_PAD
    qk = MLA_NOPE + MLA_ROPE
    rope_lo = MLA_NOPE
    w_cq_ckv = w_in[:, :MLA_Q_RANK + MLA_KV_RANK]
    w_kr = jnp.pad(w_in[:, MLA_Q_RANK + MLA_KV_RANK:], ((0, 0), (rope_lo, hp - qk)))
    w_in_pad = jnp.concatenate([w_cq_ckv, w_kr], axis=1).astype(BF16)
    cin = _mod_matmul(x, sc, sh, w_in_pad, tm=_tile(s, 1024), tn=MLA_IN_PAD, out_dtype=F32, name="mla_in_proj")
    w_uq_pad = jnp.pad(w_uq.reshape(MLA_Q_RANK, nh, qk), ((0, 0), (0, 0), (0, hp - qk)))
    w_uq_pad = w_uq_pad.reshape(MLA_Q_RANK, nh * hp).astype(BF16)
    w_ukv3 = w_ukv.reshape(MLA_KV_RANK, nh, MLA_NOPE + MLA_V)
    w_uk_pad = jnp.pad(w_ukv3[:, :, :MLA_NOPE], ((0, 0), (0, 0), (0, hp - MLA_NOPE)))
    w_uk_pad = w_uk_pad.reshape(MLA_KV_RANK, nh * hp).astype(BF16)
    w_uv = jnp.pad(w_ukv3[:, :, MLA_NOPE:], ((0, 0), (0, 0), (0, hp - MLA_V)))
    w_uv = w_uv.reshape(MLA_KV_RANK, nh * hp).astype(BF16)
    half = MLA_ROPE // 2
    freqs = ROPE_THETA ** (-jnp.arange(half, dtype=F32) / half)
    zeros = lambda n: jnp.zeros((n,), F32)
    freq_row = jnp.concatenate([zeros(rope_lo), freqs, freqs, zeros(hp - qk)])[None, :]
    sign_row = jnp.concatenate([zeros(rope_lo), -jnp.ones((half,), F32), jnp.ones((half,), F32),
                                zeros(hp - qk)])[None, :]
    q, k, v = _mla_proj(cin, positions[:, :, None], freq_row, sign_row, q_norm[None, :], kv_norm[None, :],
                        w_uq_pad, w_uk_pad, w_uv, tm=_tile(s, 512))
    attn = _attention(q, k, v, tq=_tile(s, 512))
    return _mm_res_ln(attn, w_out.astype(BF16), x, gate, lng, lnb, tm=_tile(s, 512), name="mla_out_proj")


def _sg_layer(x, sc, sh, gate, lng, lnb, w_in, b_in, ln_g, ln_b, w_s, b_s, w_out):
    s = x.shape[1]
    uv = _sg_in_proj(x, sc, sh, w_in.astype(BF16), b_in[None, :], ln_g[None, :], ln_b[None, :], tm=_tile(s, 512))
    return _sg_out(uv, w_s, b_s.T, w_out.astype(BF16), x, gate, lng, lnb, tm=_tile(s, 512))


def _dense_ffn(x, sc, sh, gate, lng, lnb, w_gate, w_up, w_down):
    s = x.shape[1]
    return _swiglu_res_ln(x, sc, sh, w_gate.astype(BF16), w_up.astype(BF16), w_down.astype(BF16),
                          gate, lng, lnb, tm=_tile(s, 512))


def _moe_ffn(x, sc, sh, gate, lng, lnb, w_router, w_gate_all, w_up_all, w_down_all, layer):
    b, s, d = x.shape
    ne = w_router.shape[1]
    ff = w_gate_all.shape[-1]
    bi = _tile(s, MOE_BLOCK)
    nblk = b * (s // bi)
    tokens = b * s
    c = MOE_COMBINE_TILES
    n_max = 2 * tokens // MOE_ROWS + nblk * ne
    n2_max = (n_max + ne * (MOE_GROUP - 1) + MOE_GROUP - 1) // MOE_GROUP
    n_bundles_max = (n_max + (c - 1) * nblk + c - 1) // c
    w_router_pad = jnp.pad(w_router, ((0, 0), (0, LANES - ne)))
    h, comb, pos, post, cnt = _router(x, sc, sh, w_router_pad, tm=bi)
    tiles, (exp2, tile2, n2), bundles = _moe_tile_lists(
        cnt[:, 0, :ne].astype(jnp.int32), n_max=n_max, n2_max=n2_max, n_bundles_max=n_bundles_max,
        rows=MOE_ROWS, group=MOE_GROUP)
    zeros = jnp.zeros((n2_max * MOE_GROUP * MOE_ROWS, d), BF16)
    xg = _moe_dispatch(tiles, h.reshape(tokens, d), post, zeros, bi=bi, rows=MOE_ROWS)
    y = _moe_experts(exp2 + layer * ne, tile2, n2, xg,
                     w_gate_all.astype(BF16).reshape(-1, d, ff), w_up_all.astype(BF16).reshape(-1, d, ff),
                     w_down_all.astype(BF16).reshape(-1, ff, d), rows2=MOE_GROUP * MOE_ROWS)
    out = _moe_combine(bundles, y, pos.reshape(tokens, LANES), comb.reshape(tokens, LANES), x.reshape(tokens, d),
                       gate, lng, lnb, bi=bi, rows=MOE_ROWS, blocks_per_batch=s // bi)
    return out.reshape(b, s, d)


def kernel(x, c, positions, ada_w, ada_b, ln_g, ln_b, ssd_w_in, ssd_conv_w, ssd_conv_b, ssd_dt_bias, ssd_a_log, ssd_d_skip, ssd_norm_w, ssd_w_out, mla_w_in, mla_q_norm, mla_kv_norm, mla_w_uq, mla_w_ukv, mla_w_out, sg_w_in, sg_b_in, sg_ln_g, sg_ln_b, sg_w_s, sg_b_s, sg_w_out, ffn_w_gate, ffn_w_up, ffn_w_down, moe_w_router, moe_w_gate, moe_w_up, moe_w_down):
    batch = x.shape[0]
    depth = ada_w.shape[0]
    c_pad = jnp.pad(c, ((0, SUBLANES - batch), (0, 0)))
    mod = _ada_mod(c_pad, ada_w, ada_b)[:, :, :batch]
    for i in range(depth):
        sh_m, sc_m, g_m, sh_f, sc_f, g_f = [mod[i, t][:, None, :] for t in range(6)]
        lng_m, lnb_m = ln_g[i, 0][None, :], ln_b[i, 0][None, :]
        lng_f, lnb_f = ln_g[i, 1][None, :], ln_b[i, 1][None, :]
        kind, j = i % 3, i // 3
        if kind == 0:
            x = _ssd_layer(x, sc_m, sh_m, g_m, lng_m, lnb_m, ssd_w_in[j], ssd_conv_w[j], ssd_conv_b[j],
                           ssd_dt_bias[j], ssd_a_log[j], ssd_d_skip[j], ssd_norm_w[j], ssd_w_out[j])
        elif kind == 1:
            x = _mla_layer(x, positions, sc_m, sh_m, g_m, lng_m, lnb_m, mla_w_in[j], mla_q_norm[j],
                           mla_kv_norm[j], mla_w_uq[j], mla_w_ukv[j], mla_w_out[j])
        else:
            x = _sg_layer(x, sc_m, sh_m, g_m, lng_m, lnb_m, sg_w_in[j], sg_b_in[j], sg_ln_g[j], sg_ln_b[j],
                          sg_w_s[j], sg_b_s[j], sg_w_out[j])
        k = i // 2
        if i % 2 == 0:
            x = _dense_ffn(x, sc_f, sh_f, g_f, lng_f, lnb_f, ffn_w_gate[k], ffn_w_up[k], ffn_w_down[k])
        else:
            x = _moe_ffn(x, sc_f, sh_f, g_f, lng_f, lnb_f, moe_w_router[k], moe_w_gate, moe_w_up, moe_w_down, k)
    return x
```

```python
import functools
import math

import jax
import jax.numpy as jnp
from jax import lax
from jax.experimental import pallas as pl
from jax.experimental.pallas import tpu as pltpu

F32 = jnp.float32
BF16 = jnp.bfloat16
HIGHEST = lax.Precision.HIGHEST

DEPTH = 4
ALPHA = (2.0 * DEPTH) ** 0.25
LN_EPS = 1e-5
RMS_EPS = 1e-6
ROPE_THETA = 10000.0

LANES = 128
SUBLANES = 8
VMEM_LIMIT = 56 * 1024 * 1024

SSD_HEAD_DIM = 64
SSD_N_GROUPS = 8
SSD_HPG = 4
SSD_D_STATE = 128
SSD_CONV = 4
SSD_CHUNK = 128
SSD_GROUP_W = SSD_HPG * SSD_HEAD_DIM

MLA_N_HEADS = 16
MLA_NOPE = 64
MLA_ROPE = 32
MLA_V = 64
MLA_Q_RANK = 512
MLA_KV_RANK = 256
MLA_HEAD_PAD = 128
MLA_IN_PAD = MLA_Q_RANK + MLA_KV_RANK + MLA_HEAD_PAD
QK_SCALE = (MLA_NOPE + MLA_ROPE) ** -0.5
LOG2E = 1.4426950408889634
NEG_BIG = -1e30

SG_GROUPS = 8
SG_CHUNK = 128

N_EXPERTS = 8
MOE_BLOCK = 1024
MOE_ROWS = 128
MOE_GROUP = 4
MOE_COMBINE_TILES = 4


def _cparams(*sem):
    return pltpu.CompilerParams(dimension_semantics=sem, vmem_limit_bytes=VMEM_LIMIT)


def _layer_norm(r):
    mu = jnp.mean(r, axis=-1, keepdims=True)
    d = r - mu
    var = jnp.mean(d * d, axis=-1, keepdims=True)
    return d * lax.rsqrt(var + LN_EPS)


def _silu(v):
    return v * jax.nn.sigmoid(v)


def _modulate(x_ref, sc_ref, sh_ref):
    return x_ref[0] * (1.0 + sc_ref[0]) + sh_ref[0]


def _res_ln(x_ref, gate_ref, y, lng_ref, lnb_ref):
    r = ALPHA * x_ref[0] + (1.0 + gate_ref[0]) * y
    return _layer_norm(r) * lng_ref[...] + lnb_ref[...]


def _ada_kernel(c_ref, w_ref, b_ref, o_ref):
    cond = _silu(c_ref[...])
    o_ref[0, 0] = jnp.dot(cond, w_ref[0], preferred_element_type=F32, precision=HIGHEST) + b_ref[0, 0]


def _ada_mod(c_pad, ada_w, ada_b):
    depth, d, _ = ada_w.shape
    rows = c_pad.shape[0]
    return pl.pallas_call(
        _ada_kernel,
        grid=(depth, 6),
        in_specs=[
            pl.BlockSpec((rows, d), lambda i, j: (0, 0)),
            pl.BlockSpec((1, d, d), lambda i, j: (i, 0, j)),
            pl.BlockSpec((1, 1, 1, d), lambda i, j: (i, j, 0, 0)),
        ],
        out_specs=pl.BlockSpec((1, 1, rows, d), lambda i, j: (i, j, 0, 0)),
        out_shape=jax.ShapeDtypeStruct((depth, 6, rows, d), F32),
        compiler_params=_cparams("arbitrary", "arbitrary"),
        name="ada_mod",
    )(c_pad, ada_w, ada_b.reshape(depth, 6, 1, d))


def _modmm_kernel(x_ref, sc_ref, sh_ref, w_ref, o_ref, h_ref):
    @pl.when(pl.program_id(2) == 0)
    def _():
        h_ref[...] = _modulate(x_ref, sc_ref, sh_ref).astype(BF16)

    o_ref[0] = jnp.dot(h_ref[...], w_ref[...], preferred_element_type=F32).astype(o_ref.dtype)


def _mod_matmul(x, sc, sh, w, *, tm, tn, out_dtype, name):
    b, s, d = x.shape
    n = w.shape[1]
    return pl.pallas_call(
        _modmm_kernel,
        grid=(b, s // tm, n // tn),
        in_specs=[
            pl.BlockSpec((1, tm, d), lambda bi, i, j: (bi, i, 0)),
            pl.BlockSpec((1, 1, d), lambda bi, i, j: (bi, 0, 0)),
            pl.BlockSpec((1, 1, d), lambda bi, i, j: (bi, 0, 0)),
            pl.BlockSpec((d, tn), lambda bi, i, j: (0, j)),
        ],
        out_specs=pl.BlockSpec((1, tm, tn), lambda bi, i, j: (bi, i, j)),
        out_shape=jax.ShapeDtypeStruct((b, s, n), out_dtype),
        scratch_shapes=[pltpu.VMEM((tm, d), BF16)],
        compiler_params=_cparams("parallel", "parallel", "arbitrary"),
        name=name,
    )(x, sc, sh, w)


def _ssd_in_kernel(x_ref, sc_ref, sh_ref, w_ref, wdt_ref, o_ref, dt_ref, h_ref):
    @pl.when(pl.program_id(2) == 0)
    def _():
        h_ref[...] = _modulate(x_ref, sc_ref, sh_ref).astype(BF16)
        dt_ref[0] = jnp.dot(h_ref[...], wdt_ref[...], preferred_element_type=F32)

    o_ref[0] = jnp.dot(h_ref[...], w_ref[...], preferred_element_type=F32).astype(o_ref.dtype)


def _ssd_in_proj(x, sc, sh, w_zxbc, w_dt, *, tm, tn):
    b, s, d = x.shape
    n = w_zxbc.shape[1]
    return pl.pallas_call(
        _ssd_in_kernel,
        grid=(b, s // tm, n // tn),
        in_specs=[
            pl.BlockSpec((1, tm, d), lambda bi, i, j: (bi, i, 0)),
            pl.BlockSpec((1, 1, d), lambda bi, i, j: (bi, 0, 0)),
            pl.BlockSpec((1, 1, d), lambda bi, i, j: (bi, 0, 0)),
            pl.BlockSpec((d, tn), lambda bi, i, j: (0, j)),
            pl.BlockSpec((d, LANES), lambda bi, i, j: (0, 0)),
        ],
        out_specs=[
            pl.BlockSpec((1, tm, tn), lambda bi, i, j: (bi, i, j)),
            pl.BlockSpec((1, tm, LANES), lambda bi, i, j: (bi, i, 0)),
        ],
        out_shape=[
            jax.ShapeDtypeStruct((b, s, n), BF16),
            jax.ShapeDtypeStruct((b, s, LANES), F32),
        ],
        scratch_shapes=[pltpu.VMEM((tm, d), BF16)],
        compiler_params=_cparams("parallel", "parallel", "arbitrary"),
        name="ssd_in_proj",
    )(x, sc, sh, w_zxbc, w_dt)


def _ssd_kernel(z_ref, x_ref, bc_ref, dt_ref, cwx_ref, cwbc_ref, cbx_ref, cbbc_ref, dtb_ref, alog_ref,
                dskip_ref, nw_ref, o_ref, prevx_ref, prevbc_ref, taps_ref, xs_ref, b_ref, c_ref, xh_ref,
                acum_ref, key_ref, wend_ref, state_ref):
    L = SSD_CHUNK
    gw = SSD_GROUP_W
    n_st = SSD_D_STATE
    slab = 2 * SUBLANES
    ci = pl.program_id(1)

    @pl.when(ci == 0)
    def _():
        prevx_ref[...] = jnp.zeros(prevx_ref.shape, BF16)
        prevbc_ref[...] = jnp.zeros(prevbc_ref.shape, BF16)
        state_ref[...] = jnp.zeros(state_ref.shape, F32)

    t_idx = lax.broadcasted_iota(jnp.int32, (L, SSD_CONV * L), 0)
    c_idx = lax.broadcasted_iota(jnp.int32, (L, SSD_CONV * L), 1)
    shift = (SSD_CONV - 1) - c_idx // L
    delta = t_idx - c_idx % L
    rot = jnp.where((delta == shift) | (delta == shift - L), 1.0, 0.0).astype(BF16)
    slab_row = lax.broadcasted_iota(jnp.int32, (slab, x_ref.shape[2]), 0)

    def conv(cur_ref, prev_ref, cw_ref, cb_ref):
        body = cur_ref[0, 0:L - slab, :]
        last = cur_ref[0, L - slab:L, :]
        prev = prev_ref[...]
        for k in range(SSD_CONV):
            w_k = cw_ref[k:k + 1, :].astype(BF16)
            swapped = jnp.where(slab_row >= slab - (SSD_CONV - 1 - k), prev, last)
            taps_ref[k * L:(k + 1) * L - slab, :] = body * w_k
            taps_ref[(k + 1) * L - slab:(k + 1) * L, :] = swapped * w_k
        prev_ref[...] = last
        return _silu(jnp.dot(rot, taps_ref[...], preferred_element_type=F32) + cb_ref[...])

    xs = conv(x_ref, prevx_ref, cwx_ref, cbx_ref)
    xs_ref[...] = xs
    xh_ref[...] = xs.astype(BF16)
    bc = conv(bc_ref, prevbc_ref, cwbc_ref, cbbc_ref)
    half = bc.shape[1] // 2
    b_ref[...] = bc[:, :half]
    c_ref[...] = bc[:, half:].astype(BF16)

    dtv = dt_ref[0] + dtb_ref[...]
    dt = jnp.maximum(dtv, 0.0) + jnp.log1p(jnp.exp(-jnp.abs(dtv)))
    a_neg = -jnp.exp(alog_ref[...])
    row = lax.broadcasted_iota(jnp.int32, (L, L), 0)
    col = lax.broadcasted_iota(jnp.int32, (L, L), 1)
    causal = col <= row
    tri = jnp.where(causal, 1.0, 0.0).astype(F32)
    a_cum = jnp.dot(tri, dt * (a_neg * LOG2E), preferred_element_type=F32, precision=HIGHEST)
    a_cum_t = a_cum.T
    key_t = a_cum_t - jnp.log2(dt.T)
    w_end_t = jnp.exp2(a_cum_t[:, L - 1:L] - key_t)

    acum_ref[...] = a_cum
    key_ref[...] = key_t
    wend_ref[...] = w_end_t

    def group(g):
        lane = lax.broadcasted_iota(jnp.int32, (L, LANES), 1)
        head_of_lane = lax.broadcasted_iota(jnp.int32, (L, gw), 1) // SSD_HEAD_DIM
        causal = lax.broadcasted_iota(jnp.int32, (L, L), 1) <= lax.broadcasted_iota(jnp.int32, (L, L), 0)
        sl = slice(g * gw, (g + 1) * gw)
        sn = slice(g * n_st, (g + 1) * n_st)
        c_g = c_ref[:, sn]
        b_g32 = b_ref[:, sn]
        b_gt = b_g32.T
        cb = lax.dot_general(c_g, b_g32.astype(BF16), (((1,), (1,)), ((), ())), preferred_element_type=F32)
        st = state_ref[g]
        y_off = jnp.dot(c_g, st.astype(BF16), preferred_element_type=F32)
        xh_g = xh_ref[:, sl]
        m_heads, bw_heads, x_heads, ea_heads = [], [], [], []
        for r in range(SSD_HPG):
            h = SSD_HPG * g + r
            a_col = jnp.broadcast_to(acum_ref[:, h:h + 1], (L, L))
            decay_dt = jnp.exp2(jnp.where(causal, a_col - key_ref[h:h + 1, :], NEG_BIG))
            m_heads.append((cb * decay_dt).astype(BF16))
            bw_heads.append((b_gt * wend_ref[h:h + 1, :]).astype(BF16))
            x_heads.append(jnp.where(head_of_lane == r, xh_g, jnp.zeros_like(xh_g)))
            ea_heads.append(jnp.exp2(a_col))
        x_stack = jnp.concatenate(x_heads, axis=0)
        y_diag = jnp.dot(jnp.concatenate(m_heads, axis=1), x_stack, preferred_element_type=F32)
        new_st = jnp.dot(jnp.concatenate(bw_heads, axis=1), x_stack, preferred_element_type=F32)
        ea_g = jnp.concatenate([jnp.where(lane < SSD_HEAD_DIM, ea_heads[0], ea_heads[1]),
                                jnp.where(lane < SSD_HEAD_DIM, ea_heads[2], ea_heads[3])], axis=1)
        y = y_diag + y_off * ea_g + xs_ref[:, sl] * dskip_ref[:, sl]
        z_g = z_ref[0, :, sl].astype(F32)
        t = y * _silu(z_g)
        ms = jnp.mean(t * t, axis=-1, keepdims=True)
        o_ref[0, :, sl] = (t * lax.rsqrt(ms + RMS_EPS) * nw_ref[:, sl]).astype(o_ref.dtype)
        state_ref[g] = st * ea_g[L - 1:L, :] + new_st

    for g in range(SSD_N_GROUPS):
        group(g)


def _ssd_scan(zxbc, dt_raw, conv_w, conv_b, dt_bias, a_log, d_skip, norm_w):
    b, s, _ = zxbc.shape
    L = SSD_CHUNK
    di = SSD_N_GROUPS * SSD_GROUP_W
    blk = lambda j: pl.BlockSpec((1, L, di), lambda bi, ci, j=j: (bi, ci, j))
    vec = lambda w: pl.BlockSpec((1, w), lambda bi, ci: (0, 0))
    cw = lambda j: pl.BlockSpec((SSD_CONV, di), lambda bi, ci, j=j: (0, j))
    cb = lambda j: pl.BlockSpec((1, di), lambda bi, ci, j=j: (0, j))
    return pl.pallas_call(
        _ssd_kernel,
        grid=(b, s // L),
        in_specs=[
            blk(0), blk(1), blk(2),
            pl.BlockSpec((1, L, LANES), lambda bi, ci: (bi, ci, 0)),
            cw(0), cw(1), cb(0), cb(1),
            vec(LANES), vec(LANES), vec(di), vec(di),
        ],
        out_specs=pl.BlockSpec((1, L, di), lambda bi, ci: (bi, ci, 0)),
        out_shape=jax.ShapeDtypeStruct((b, s, di), BF16),
        scratch_shapes=[
            pltpu.VMEM((2 * SUBLANES, di), BF16),
            pltpu.VMEM((2 * SUBLANES, di), BF16),
            pltpu.VMEM((SSD_CONV * L, di), BF16),
            pltpu.VMEM((L, di), F32),
            pltpu.VMEM((L, di // 2), F32),
            pltpu.VMEM((L, di // 2), BF16),
            pltpu.VMEM((L, di), BF16),
            pltpu.VMEM((L, LANES), F32),
            pltpu.VMEM((LANES, L), F32),
            pltpu.VMEM((LANES, L), F32),
            pltpu.VMEM((SSD_N_GROUPS, SSD_D_STATE, SSD_GROUP_W), F32),
        ],
        compiler_params=_cparams("arbitrary", "arbitrary"),
        name="ssd_scan",
    )(zxbc, zxbc, zxbc, dt_raw, conv_w, conv_w, conv_b, conv_b, dt_bias, a_log, d_skip, norm_w)


def _mm_res_ln_kernel(a_ref, w_ref, x_ref, gate_ref, lng_ref, lnb_ref, o_ref):
    y = jnp.dot(a_ref[0], w_ref[...], preferred_element_type=F32)
    o_ref[0] = _res_ln(x_ref, gate_ref, y, lng_ref, lnb_ref)


def _mm_res_ln(a, w, x, gate, lng, lnb, *, tm, name):
    b, s, d = x.shape
    k = a.shape[2]
    return pl.pallas_call(
        _mm_res_ln_kernel,
        grid=(b, s // tm),
        in_specs=[
            pl.BlockSpec((1, tm, k), lambda bi, i: (bi, i, 0)),
            pl.BlockSpec((k, d), lambda bi, i: (0, 0)),
            pl.BlockSpec((1, tm, d), lambda bi, i: (bi, i, 0)),
            pl.BlockSpec((1, 1, d), lambda bi, i: (bi, 0, 0)),
            pl.BlockSpec((1, d), lambda bi, i: (0, 0)),
            pl.BlockSpec((1, d), lambda bi, i: (0, 0)),
        ],
        out_specs=pl.BlockSpec((1, tm, d), lambda bi, i: (bi, i, 0)),
        out_shape=jax.ShapeDtypeStruct((b, s, d), F32),
        compiler_params=_cparams("parallel", "parallel"),
        name=name,
    )(a, w, x, gate, lng, lnb)


def _mla_proj_kernel(cin_ref, pos_ref, freq_ref, sign_ref, qn_ref, kvn_ref, wuq_ref, wuk_ref, wuv_ref,
                     q_ref, k_ref, v_ref):
    cin = cin_ref[0]
    tm = cin.shape[0]
    hp = MLA_HEAD_PAD

    def rms(v, w_ref):
        return v * lax.rsqrt(jnp.mean(v * v, axis=-1, keepdims=True) + RMS_EPS) * w_ref[...]

    cq = rms(cin[:, :MLA_Q_RANK], qn_ref).astype(BF16)
    ckv = rms(cin[:, MLA_Q_RANK:MLA_Q_RANK + MLA_KV_RANK], kvn_ref).astype(BF16)
    k_rope = cin[:, MLA_Q_RANK + MLA_KV_RANK:]
    q = jnp.dot(cq, wuq_ref[...], preferred_element_type=F32)
    k = jnp.dot(ckv, wuk_ref[...], preferred_element_type=F32)
    ones_col = jnp.where(lax.broadcasted_iota(jnp.int32, (1, wuv_ref.shape[1]), 1) % hp == MLA_V, 1.0, 0.0)
    v_ref[0] = (jnp.dot(ckv, wuv_ref[...], preferred_element_type=F32) + ones_col).astype(v_ref.dtype)

    ang = pos_ref[0].astype(F32) * freq_ref[...]
    cos = jnp.cos(ang)
    sin = jnp.sin(ang) * sign_ref[...]
    lane = lax.broadcasted_iota(jnp.int32, (tm, hp), 1)
    first_half = lane < MLA_NOPE + MLA_ROPE // 2

    def rope(xh):
        swapped = jnp.where(first_half,
                            pltpu.roll(xh, hp - MLA_ROPE // 2, 1),
                            pltpu.roll(xh, MLA_ROPE // 2, 1))
        return xh * cos + swapped * sin

    kr = rope(k_rope)
    for h in range(MLA_N_HEADS):
        sl = slice(h * hp, (h + 1) * hp)
        q_ref[0, :, sl] = (rope(q[:, sl]) * (QK_SCALE * LOG2E)).astype(q_ref.dtype)
        k_ref[0, :, sl] = (k[:, sl] + kr).astype(k_ref.dtype)


def _mla_proj(cin, pos, freq, sign, q_norm, kv_norm, w_uq, w_uk, w_uv, *, tm):
    b, s, n_in = cin.shape
    nq = w_uq.shape[1]
    nv = w_uv.shape[1]
    full = lambda a: pl.BlockSpec(a.shape, lambda bi, i: (0,) * a.ndim)
    return pl.pallas_call(
        _mla_proj_kernel,
        grid=(b, s // tm),
        in_specs=[
            pl.BlockSpec((1, tm, n_in), lambda bi, i: (bi, i, 0)),
            pl.BlockSpec((1, tm, 1), lambda bi, i: (bi, i, 0)),
            full(freq), full(sign), full(q_norm), full(kv_norm), full(w_uq), full(w_uk), full(w_uv),
        ],
        out_specs=[
            pl.BlockSpec((1, tm, nq), lambda bi, i: (bi, i, 0)),
            pl.BlockSpec((1, tm, nq), lambda bi, i: (bi, i, 0)),
            pl.BlockSpec((1, tm, nv), lambda bi, i: (bi, i, 0)),
        ],
        out_shape=[
            jax.ShapeDtypeStruct((b, s, nq), BF16),
            jax.ShapeDtypeStruct((b, s, nq), BF16),
            jax.ShapeDtypeStruct((b, s, nv), BF16),
        ],
        compiler_params=_cparams("parallel", "parallel"),
        name="mla_proj",
    )(cin, pos, freq, sign, q_norm, kv_norm, w_uq, w_uk, w_uv)


def _attn_kernel(q_ref, k_ref, v_ref, o_ref, acc_ref, s0_ref, s1_ref, *, tq, n_kb):
    qi = pl.program_id(2)
    hp = MLA_HEAD_PAD
    def scores(kb, s_ref, masked=True):
        start = pl.multiple_of(jnp.minimum(kb, n_kb - 1) * tq, tq)
        if masked:
            key_minus_query = (lax.broadcasted_iota(jnp.int32, (tq, tq), 1)
                               - lax.broadcasted_iota(jnp.int32, (tq, tq), 0))
            keep = key_minus_query <= (qi - kb) * tq
        for hh in range(2):
            q = q_ref[0, :, hh * hp:(hh + 1) * hp]
            k = k_ref[0, pl.ds(start, tq), hh * hp:(hh + 1) * hp]
            s = lax.dot_general(q, k, (((1,), (1,)), ((), ())), preferred_element_type=F32)
            s_ref[hh] = jnp.where(keep, s, NEG_BIG) if masked else s

    def softmax_pv(kb, s_ref, m_prev):
        start = pl.multiple_of(jnp.minimum(kb, n_kb - 1) * tq, tq)
        m_out = []
        for hh in range(2):
            v = v_ref[0, pl.ds(start, tq), hh * hp:(hh + 1) * hp]
            s = s_ref[hh]
            m_new = jnp.maximum(m_prev[hh], jnp.max(s, axis=1, keepdims=True))
            alpha = jnp.exp2(m_prev[hh] - m_new)
            p = jnp.exp2((s - m_new).astype(BF16))
            acc_ref[hh] = acc_ref[hh] * alpha + jnp.dot(p, v, preferred_element_type=F32)
            m_out.append(m_new)
        return tuple(m_out)

    acc_ref[...] = jnp.zeros(acc_ref.shape, F32)
    m_init = jnp.full((tq, 1), NEG_BIG, F32)
    scores(0, s0_ref)

    def two_blocks(t, m, mask_ahead):
        m = softmax_pv(2 * t, s0_ref, m)
        scores(2 * t + 1, s1_ref, masked=False)
        m = softmax_pv(2 * t + 1, s1_ref, m)
        scores(2 * t + 2, s0_ref, masked=mask_ahead)
        return m

    n_pairs = (qi + 2) // 2
    n_plain = jnp.maximum(n_pairs - 2, 0)
    m = lax.fori_loop(0, n_plain, functools.partial(two_blocks, mask_ahead=False), (m_init, m_init))
    m = lax.fori_loop(n_plain, n_pairs - 1, functools.partial(two_blocks, mask_ahead=True), m)
    m = softmax_pv(2 * n_pairs - 2, s0_ref, m)
    scores(2 * n_pairs - 1, s1_ref)
    softmax_pv(2 * n_pairs - 1, s1_ref, m)
    out_a = acc_ref[0] / acc_ref[0][:, MLA_V:MLA_V + 1]
    out_b = acc_ref[1] / acc_ref[1][:, MLA_V:MLA_V + 1]
    lane = lax.broadcasted_iota(jnp.int32, (tq, hp), 1)
    o_ref[0] = jnp.where(lane < MLA_V, out_a, pltpu.roll(out_b, MLA_V, 1)).astype(o_ref.dtype)


def _attention(q, k, v, *, tq):
    b, s, _ = q.shape
    pairs = MLA_N_HEADS // 2
    return pl.pallas_call(
        functools.partial(_attn_kernel, tq=tq, n_kb=s // tq),
        grid=(b, pairs, s // tq),
        in_specs=[
            pl.BlockSpec((1, tq, 2 * MLA_HEAD_PAD), lambda bi, j, i: (bi, i, j)),
            pl.BlockSpec((1, s, 2 * MLA_HEAD_PAD), lambda bi, j, i: (bi, 0, j)),
            pl.BlockSpec((1, s, 2 * MLA_HEAD_PAD), lambda bi, j, i: (bi, 0, j)),
        ],
        out_specs=pl.BlockSpec((1, tq, 2 * MLA_V), lambda bi, j, i: (bi, i, j)),
        out_shape=jax.ShapeDtypeStruct((b, s, MLA_N_HEADS * MLA_V), BF16),
        scratch_shapes=[pltpu.VMEM((2, tq, MLA_HEAD_PAD), F32), pltpu.VMEM((2, tq, tq), F32),
                        pltpu.VMEM((2, tq, tq), F32)],
        compiler_params=_cparams("parallel", "parallel", "arbitrary"),
        name="mla_attention",
    )(q, k, v)


def _sg_in_kernel(x_ref, sc_ref, sh_ref, w_ref, b_ref, lng_ref, lnb_ref, o_ref, h_ref):
    j = pl.program_id(2)

    @pl.when(j == 0)
    def _():
        h_ref[...] = _modulate(x_ref, sc_ref, sh_ref).astype(BF16)

    pre = jnp.dot(h_ref[...], w_ref[...], preferred_element_type=F32) + b_ref[...]
    y = jax.nn.gelu(pre.astype(BF16))

    @pl.when(j == 0)
    def _():
        o_ref[0] = y.astype(o_ref.dtype)

    @pl.when(j == 1)
    def _():
        o_ref[0] = (_layer_norm(y.astype(F32)) * lng_ref[...] + lnb_ref[...]).astype(o_ref.dtype)


def _sg_in_proj(x, sc, sh, w, bias, lng, lnb, *, tm):
    b, s, d = x.shape
    n = w.shape[1]
    tn = n // 2
    return pl.pallas_call(
        _sg_in_kernel,
        grid=(b, s // tm, 2),
        in_specs=[
            pl.BlockSpec((1, tm, d), lambda bi, i, j: (bi, i, 0)),
            pl.BlockSpec((1, 1, d), lambda bi, i, j: (bi, 0, 0)),
            pl.BlockSpec((1, 1, d), lambda bi, i, j: (bi, 0, 0)),
            pl.BlockSpec((d, tn), lambda bi, i, j: (0, j)),
            pl.BlockSpec((1, tn), lambda bi, i, j: (0, j)),
            pl.BlockSpec((1, tn), lambda bi, i, j: (0, 0)),
            pl.BlockSpec((1, tn), lambda bi, i, j: (0, 0)),
        ],
        out_specs=pl.BlockSpec((1, tm, tn), lambda bi, i, j: (bi, i, j)),
        out_shape=jax.ShapeDtypeStruct((b, s, n), BF16),
        scratch_shapes=[pltpu.VMEM((tm, d), BF16)],
        compiler_params=_cparams("parallel", "parallel", "arbitrary"),
        name="sg_in_proj",
    )(x, sc, sh, w, bias, lng, lnb)


def _sg_out_kernel(u_ref, v_ref, ws_ref, bst_ref, w_ref, x_ref, gate_ref, lng_ref, lnb_ref, o_ref, gated_ref):
    tm = u_ref.shape[1]
    L = SG_CHUNK
    gd = u_ref.shape[2] // SG_GROUPS
    row = lax.broadcasted_iota(jnp.int32, (L, L), 0)
    col = lax.broadcasted_iota(jnp.int32, (L, L), 1)
    causal = col <= row
    for g in range(SG_GROUPS):
        ws = jnp.where(causal, ws_ref[g], 0.0).astype(BF16)
        bias = bst_ref[:, g:g + 1]
        for c in range(tm // L):
            rs = slice(c * L, (c + 1) * L)
            cs = slice(g * gd, (g + 1) * gd)
            mixed = jnp.dot(ws, v_ref[0, rs, cs], preferred_element_type=F32) + bias
            gated_ref[rs, cs] = (u_ref[0, rs, cs].astype(F32) * mixed).astype(BF16)
    y = jnp.dot(gated_ref[...], w_ref[...], preferred_element_type=F32)
    o_ref[0] = _res_ln(x_ref, gate_ref, y, lng_ref, lnb_ref)


def _sg_out(uv, w_s, b_s_t, w_out, x, gate, lng, lnb, *, tm):
    b, s, d = x.shape
    sgd = uv.shape[2] // 2
    full = lambda a: pl.BlockSpec(a.shape, lambda bi, i: (0,) * a.ndim)
    return pl.pallas_call(
        _sg_out_kernel,
        grid=(b, s // tm),
        in_specs=[
            pl.BlockSpec((1, tm, sgd), lambda bi, i: (bi, i, 0)),
            pl.BlockSpec((1, tm, sgd), lambda bi, i: (bi, i, 1)),
            full(w_s), full(b_s_t), full(w_out),
            pl.BlockSpec((1, tm, d), lambda bi, i: (bi, i, 0)),
            pl.BlockSpec((1, 1, d), lambda bi, i: (bi, 0, 0)),
            pl.BlockSpec((1, d), lambda bi, i: (0, 0)),
            pl.BlockSpec((1, d), lambda bi, i: (0, 0)),
        ],
        out_specs=pl.BlockSpec((1, tm, d), lambda bi, i: (bi, i, 0)),
        out_shape=jax.ShapeDtypeStruct((b, s, d), F32),
        scratch_shapes=[pltpu.VMEM((tm, sgd), BF16)],
        compiler_params=_cparams("parallel", "parallel"),
        name="sg_out",
    )(uv, uv, w_s, b_s_t, w_out, x, gate, lng, lnb)


def _swiglu_kernel(x_ref, sc_ref, sh_ref, wg_ref, wu_ref, wd_ref, gate_ref, lng_ref, lnb_ref, o_ref, *, n_split):
    h = _modulate(x_ref, sc_ref, sh_ref).astype(BF16)
    tf = wg_ref.shape[1] // n_split
    acc = jnp.zeros(o_ref.shape[1:], F32)
    for f in range(n_split):
        fs = slice(f * tf, (f + 1) * tf)
        g = jnp.dot(h, wg_ref[:, fs], preferred_element_type=F32)
        u = jnp.dot(h, wu_ref[:, fs], preferred_element_type=F32)
        acc = acc + jnp.dot((_silu(g) * u).astype(BF16), wd_ref[fs, :], preferred_element_type=F32)
    o_ref[0] = _res_ln(x_ref, gate_ref, acc, lng_ref, lnb_ref)


def _swiglu_res_ln(x, sc, sh, wg, wu, wd, gate, lng, lnb, *, tm):
    b, s, d = x.shape
    whole = lambda a: pl.BlockSpec(a.shape, lambda bi, i: (0, 0))
    return pl.pallas_call(
        functools.partial(_swiglu_kernel, n_split=2),
        grid=(b, s // tm),
        in_specs=[
            pl.BlockSpec((1, tm, d), lambda bi, i: (bi, i, 0)),
            pl.BlockSpec((1, 1, d), lambda bi, i: (bi, 0, 0)),
            pl.BlockSpec((1, 1, d), lambda bi, i: (bi, 0, 0)),
            whole(wg), whole(wu), whole(wd),
            pl.BlockSpec((1, 1, d), lambda bi, i: (bi, 0, 0)),
            pl.BlockSpec((1, d), lambda bi, i: (0, 0)),
            pl.BlockSpec((1, d), lambda bi, i: (0, 0)),
        ],
        out_specs=pl.BlockSpec((1, tm, d), lambda bi, i: (bi, i, 0)),
        out_shape=jax.ShapeDtypeStruct((b, s, d), F32),
        compiler_params=_cparams("parallel", "parallel"),
        name="swiglu_res_ln",
    )(x, sc, sh, wg, wu, wd, gate, lng, lnb)


def _router_kernel(x_ref, sc_ref, sh_ref, wr_ref, h_ref, comb_ref, pos_ref, post_ref, cnt_ref):
    h = _modulate(x_ref, sc_ref, sh_ref)
    h_ref[0] = h.astype(h_ref.dtype)
    logits = jnp.dot(h, wr_ref[...], preferred_element_type=F32, precision=HIGHEST)
    tm = logits.shape[0]
    lane = lax.broadcasted_iota(jnp.int32, logits.shape, 1).astype(F32)
    lg = jnp.where(lane < N_EXPERTS, logits, -jnp.inf)
    m1 = jnp.max(lg, axis=1, keepdims=True)
    i1 = jnp.min(jnp.where(lg == m1, lane, float(LANES)), axis=1, keepdims=True)
    lg2 = jnp.where(lane == i1, -jnp.inf, lg)
    m2 = jnp.max(lg2, axis=1, keepdims=True)
    i2 = jnp.min(jnp.where(lg2 == m2, lane, float(LANES)), axis=1, keepdims=True)
    e2 = jnp.exp(m2 - m1)
    den = 1.0 + e2
    comb_ref[0] = jnp.where(lane == i1, 1.0 / den, 0.0) + jnp.where(lane == i2, e2 / den, 0.0)
    sel = (lane == i1) | (lane == i2)
    earlier = (lax.broadcasted_iota(jnp.int32, (tm, tm), 1) < lax.broadcasted_iota(jnp.int32, (tm, tm), 0))
    sel_f = jnp.where(sel, 1.0, 0.0)
    rank = jnp.dot(jnp.where(earlier, 1.0, 0.0).astype(BF16), sel_f.astype(BF16), preferred_element_type=F32)
    pos = jnp.where(sel, rank, -1.0)
    pos_ref[0] = pos
    post_ref[0] = pos.T[0:SUBLANES, :]
    cnt_ref[0] = jnp.sum(sel_f, axis=0, keepdims=True)


def _router(x, sc, sh, w_router_pad, *, tm):
    b, s, d = x.shape
    nbs = s // tm
    tok = lambda w: pl.BlockSpec((1, tm, w), lambda bi, i: (bi, i, 0))
    return pl.pallas_call(
        _router_kernel,
        grid=(b, nbs),
        in_specs=[
            tok(d),
            pl.BlockSpec((1, 1, d), lambda bi, i: (bi, 0, 0)),
            pl.BlockSpec((1, 1, d), lambda bi, i: (bi, 0, 0)),
            pl.BlockSpec((d, LANES), lambda bi, i: (0, 0)),
        ],
        out_specs=[
            tok(d), tok(LANES), tok(LANES),
            pl.BlockSpec((1, SUBLANES, tm), lambda bi, i: (bi * nbs + i, 0, 0)),
            pl.BlockSpec((1, 1, LANES), lambda bi, i: (bi * nbs + i, 0, 0)),
        ],
        out_shape=[
            jax.ShapeDtypeStruct((b, s, d), BF16),
            jax.ShapeDtypeStruct((b, s, LANES), F32),
            jax.ShapeDtypeStruct((b, s, LANES), F32),
            jax.ShapeDtypeStruct((b * nbs, SUBLANES, tm), F32),
            jax.ShapeDtypeStruct((b * nbs, 1, LANES), F32),
        ],
        compiler_params=_cparams("parallel", "parallel"),
        name="moe_router",
    )(x, sc, sh, w_router_pad)


def _moe_dispatch_kernel(blk_ref, exp_ref, sub_ref, dst_ref, n_ref, h_ref, post_ref, o_ref, *, n_max):
    s = pl.program_id(0)

    @pl.when((s < n_ref[0]) | ((s >= n_max) & (s - n_max < n_ref[1])))
    def _():
        rows = o_ref.shape[0]
        pos_row = post_ref[0, pl.ds(exp_ref[s], 1), :]
        want = sub_ref[s] * rows + lax.broadcasted_iota(jnp.int32, (rows, 1), 0)
        onehot = jnp.where(pos_row == want.astype(F32), 1.0, 0.0).astype(BF16)
        o_ref[...] = jnp.dot(onehot, h_ref[...], preferred_element_type=F32).astype(o_ref.dtype)


def _moe_dispatch(lists, h2d, post, *, bi, rows, n_max, n_rows):
    blk, exp, sub, dst, n = lists
    d = h2d.shape[1]
    grid_spec = pltpu.PrefetchScalarGridSpec(
        num_scalar_prefetch=5,
        grid=(blk.shape[0],),
        in_specs=[
            pl.BlockSpec((bi, d), lambda s, blk, exp, sub, dst, n: (blk[s], 0)),
            pl.BlockSpec((1, SUBLANES, bi), lambda s, blk, exp, sub, dst, n: (blk[s], 0, 0)),
        ],
        out_specs=pl.BlockSpec((rows, d), lambda s, blk, exp, sub, dst, n: (dst[s], 0)),
    )
    return pl.pallas_call(
        functools.partial(_moe_dispatch_kernel, n_max=n_max),
        grid_spec=grid_spec,
        out_shape=jax.ShapeDtypeStruct((n_rows, d), BF16),
        compiler_params=_cparams("arbitrary"),
        name="moe_dispatch",
    )(blk, exp, sub, dst, n, h2d, post)


def _moe_expert_kernel(exp_ref, tile_ref, n_ref, xg_ref, wg_ref, wu_ref, wd_ref, o_ref, *, n_split):
    del exp_ref, tile_ref
    s = pl.program_id(0)

    @pl.when(s < n_ref[0])
    def _():
        xg = xg_ref[...]
        ff = wg_ref.shape[2]
        tf = ff // n_split
        acc = jnp.zeros(o_ref.shape, F32)
        for f in range(n_split):
            fs = slice(f * tf, (f + 1) * tf)
            g = jnp.dot(xg, wg_ref[0, :, fs], preferred_element_type=F32)
            u = jnp.dot(xg, wu_ref[0, :, fs], preferred_element_type=F32)
            acc = acc + jnp.dot((_silu(g) * u).astype(BF16), wd_ref[0, fs, :], preferred_element_type=F32)
        o_ref[...] = acc.astype(o_ref.dtype)


def _moe_experts(exp2, tile2, n2, xg, wg, wu, wd, *, rows2):
    _, d, ff = wg.shape
    grid_spec = pltpu.PrefetchScalarGridSpec(
        num_scalar_prefetch=3,
        grid=(exp2.shape[0],),
        in_specs=[
            pl.BlockSpec((rows2, d), lambda s, exp, tile, n: (tile[s], 0)),
            pl.BlockSpec((1, d, ff), lambda s, exp, tile, n: (exp[s], 0, 0)),
            pl.BlockSpec((1, d, ff), lambda s, exp, tile, n: (exp[s], 0, 0)),
            pl.BlockSpec((1, ff, d), lambda s, exp, tile, n: (exp[s], 0, 0)),
        ],
        out_specs=pl.BlockSpec((rows2, d), lambda s, exp, tile, n: (tile[s], 0)),
    )
    return pl.pallas_call(
        functools.partial(_moe_expert_kernel, n_split=2),
        grid_spec=grid_spec,
        out_shape=jax.ShapeDtypeStruct(xg.shape, BF16),
        compiler_params=_cparams("arbitrary"),
        name="moe_experts",
    )(exp2, tile2, n2, xg, wg, wu, wd)


def _moe_combine_kernel(*refs):
    c = MOE_COMBINE_TILES
    first_ref, last_ref, n_ref = refs[1:4]
    slot_refs = refs[4:4 + 3 * c]
    y_refs = refs[4 + 3 * c:4 + 4 * c]
    pos_ref, comb_ref, x_ref, gate_ref, lng_ref, lnb_ref, o_ref, acc_ref = refs[4 + 4 * c:]
    s = pl.program_id(0)

    @pl.when(s < n_ref[0])
    def _():
        @pl.when(first_ref[s] == 1)
        def _():
            acc_ref[...] = jnp.zeros(acc_ref.shape, F32)

        rows = y_refs[0].shape[0]
        pos = pos_ref[...]
        comb = comb_ref[...]
        lane = lax.broadcasted_iota(jnp.int32, pos.shape, 1)
        row_id = lax.broadcasted_iota(jnp.int32, (1, rows), 1)

        def scatter(e, sub):
            pos_e = jnp.max(jnp.where(lane == e, pos, -2.0), axis=1, keepdims=True)
            w_e = jnp.sum(jnp.where(lane == e, comb, 0.0), axis=1, keepdims=True)
            return jnp.where(pos_e == (sub * rows + row_id).astype(F32), w_e, 0.0).astype(BF16)

        scat = jnp.concatenate([scatter(slot_refs[3 * i][s], slot_refs[3 * i + 1][s]) for i in range(c)], axis=1)
        y = jnp.concatenate([r[...] for r in y_refs], axis=0)
        acc_ref[...] += jnp.dot(scat, y, preferred_element_type=F32)

        @pl.when(last_ref[s] == 1)
        def _():
            r = ALPHA * x_ref[...] + (1.0 + gate_ref[0]) * acc_ref[...]
            o_ref[...] = _layer_norm(r) * lng_ref[...] + lnb_ref[...]


def _moe_combine(bundles, y, pos2d, comb2d, x2d, gate, lng, lnb, *, bi, rows, blocks_per_batch):
    d = x2d.shape[1]
    c = MOE_COMBINE_TILES
    by_blk = lambda w: pl.BlockSpec((bi, w), lambda s, blk, *_: (blk[s], 0))
    y_spec = lambda i: pl.BlockSpec((rows, d), lambda s, *pf: (pf[4 + 3 * i + 2][s], 0))
    grid_spec = pltpu.PrefetchScalarGridSpec(
        num_scalar_prefetch=len(bundles),
        grid=(bundles[0].shape[0],),
        in_specs=[y_spec(i) for i in range(c)] + [
            by_blk(LANES), by_blk(LANES), by_blk(d),
            pl.BlockSpec((1, 1, d), lambda s, blk, *_: (blk[s] // blocks_per_batch, 0, 0)),
            pl.BlockSpec((1, d), lambda s, *_: (0, 0)),
            pl.BlockSpec((1, d), lambda s, *_: (0, 0)),
        ],
        out_specs=by_blk(d),
        scratch_shapes=[pltpu.VMEM((bi, d), F32)],
    )
    return pl.pallas_call(
        _moe_combine_kernel,
        grid_spec=grid_spec,
        out_shape=jax.ShapeDtypeStruct(x2d.shape, F32),
        compiler_params=_cparams("arbitrary"),
        name="moe_combine",
    )(*bundles, *([y] * c), pos2d, comb2d, x2d, gate, lng, lnb)


def _moe_tile_lists(cnt, *, n_max, n2_max, n_bundles_max, rows, group):
    nblk, ne = cnt.shape
    i32 = jnp.int32
    nt = (cnt + rows - 1) // rows
    nt_e_pad = (nt.sum(axis=0) + group - 1) // group * group
    e_end = jnp.cumsum(nt_e_pad)
    dst0 = ((e_end - nt_e_pad)[None, :] + jnp.cumsum(nt, axis=0) - nt).reshape(-1)
    flat = nt.reshape(-1)
    ends = jnp.cumsum(flat)
    n_tiles = ends[-1]
    count_le = lambda bounds, v: jnp.sum(bounds[None, :] <= v[:, None], axis=1).astype(i32)
    slot = jnp.minimum(jnp.arange(n_max, dtype=i32), n_tiles - 1)
    seg = count_le(ends, slot)
    blk = seg // ne
    exp = seg % ne
    sub = slot - (ends[seg] - flat[seg])
    dst = dst0[seg] + sub
    no_match = jnp.int32(1 << 20)
    pad_e = nt_e_pad - nt.sum(axis=0)
    pad_end = jnp.cumsum(pad_e)
    n_pad = pad_end[-1]
    q = jnp.minimum(jnp.arange(ne * (group - 1), dtype=i32), jnp.maximum(n_pad - 1, 0))
    q_e = jnp.minimum(count_le(pad_end, q), ne - 1)
    pad_dst = e_end[q_e] - pad_end[q_e] + q
    pad_dst = jnp.where(n_pad > 0, pad_dst, dst[-1])
    cat = lambda a, b: jnp.concatenate([a.astype(i32), b.astype(i32)])
    tiles = (cat(blk, jnp.broadcast_to(blk[-1], q.shape)), cat(exp, jnp.zeros_like(q)),
             cat(sub, jnp.broadcast_to(no_match, q.shape)), cat(dst, pad_dst),
             jnp.stack([n_tiles, n_pad]).astype(i32))
    n2 = e_end[-1] // group
    tile2 = jnp.minimum(jnp.arange(n2_max, dtype=i32), n2 - 1)
    exp2 = jnp.minimum(count_le(e_end, tile2 * group), ne - 1)
    experts = (exp2.astype(i32), tile2.astype(i32), n2.astype(i32)[None])
    c = MOE_COMBINE_TILES
    nt_blk = nt.sum(axis=1)
    blk_end = ends.reshape(nblk, ne)[:, -1]
    blk_begin = blk_end - nt_blk
    nb_blk = (nt_blk + c - 1) // c
    b_end = jnp.cumsum(nb_blk)
    n_bundles = b_end[-1]
    bslot = jnp.minimum(jnp.arange(n_bundles_max, dtype=i32), n_bundles - 1)
    bblk = count_le(b_end, bslot)
    idx = bslot - (b_end[bblk] - nb_blk[bblk])
    t0 = blk_begin[bblk] + c * idx
    no_match = jnp.int32(1 << 20)
    bundles = [bblk.astype(i32), (idx == 0).astype(i32), (idx == nb_blk[bblk] - 1).astype(i32),
               n_bundles.astype(i32)[None]]
    for i in range(c):
        real = t0 + i < blk_end[bblk]
        t = jnp.where(real, t0 + i, t0)
        bundles += [exp[t].astype(i32), jnp.where(real, sub[t], no_match).astype(i32), dst[t].astype(i32)]
    return tiles, experts, tuple(bundles)


def _tile(s, pref):
    return min(pref, s)


def _ssd_layer(x, sc, sh, gate, lng, lnb, w_in, conv_w, conv_b, dt_bias, a_log, d_skip, norm_w, w_out):
    s = x.shape[1]
    n_heads = dt_bias.shape[0]
    d_inner = n_heads * SSD_HEAD_DIM
    n_zxbc = w_in.shape[1] - n_heads
    pad = LANES - n_heads
    w_zxbc = w_in[:, :n_zxbc].astype(BF16)
    w_dt = jnp.pad(w_in[:, n_zxbc:], ((0, 0), (0, pad))).astype(BF16)
    zxbc, dt_raw = _ssd_in_proj(x, sc, sh, w_zxbc, w_dt, tm=_tile(s, 1024), tn=d_inner)
    yg = _ssd_scan(
        zxbc, dt_raw, conv_w, conv_b[None, :],
        jnp.pad(dt_bias, (0, pad))[None, :], jnp.pad(a_log, (0, pad))[None, :],
        jnp.repeat(d_skip, SSD_HEAD_DIM)[None, :], norm_w[None, :])
    return _mm_res_ln(yg, w_out.astype(BF16), x, gate, lng, lnb, tm=_tile(s, 512), name="ssd_out_proj")


def _mla_layer(x, positions, sc, sh, gate, lng, lnb, w_in, q_norm, kv_norm, w_uq, w_ukv, w_out):
    s = x.shape[1]
    d = x.shape[2]
    nh, hp = MLA_N_HEADS, MLA_HEAD_PAD
    qk = MLA_NOPE + MLA_ROPE
    rope_lo = MLA_NOPE
    w_cq_ckv = w_in[:, :MLA_Q_RANK + MLA_KV_RANK]
    w_kr = jnp.pad(w_in[:, MLA_Q_RANK + MLA_KV_RANK:], ((0, 0), (rope_lo, hp - qk)))
    w_in_pad = jnp.concatenate([w_cq_ckv, w_kr], axis=1).astype(BF16)
    cin = _mod_matmul(x, sc, sh, w_in_pad, tm=_tile(s, 1024), tn=MLA_IN_PAD, out_dtype=F32, name="mla_in_proj")
    w_uq_pad = jnp.pad(w_uq.reshape(MLA_Q_RANK, nh, qk), ((0, 0), (0, 0), (0, hp - qk)))
    w_uq_pad = w_uq_pad.reshape(MLA_Q_RANK, nh * hp).astype(BF16)
    w_ukv3 = w_ukv.reshape(MLA_KV_RANK, nh, MLA_NOPE + MLA_V)
    w_uk_pad = jnp.pad(w_ukv3[:, :, :MLA_NOPE], ((0, 0), (0, 0), (0, hp - MLA_NOPE)))
    w_uk_pad = w_uk_pad.reshape(MLA_KV_RANK, nh * hp).astype(BF16)
    w_uv = jnp.pad(w_ukv3[:, :, MLA_NOPE:], ((0, 0), (0, 0), (0, hp - MLA_V)))
    w_uv = w_uv.reshape(MLA_KV_RANK, nh * hp).astype(BF16)
    half = MLA_ROPE // 2
    freqs = ROPE_THETA ** (-jnp.arange(half, dtype=F32) / half)
    zeros = lambda n: jnp.zeros((n,), F32)
    freq_row = jnp.concatenate([zeros(rope_lo), freqs, freqs, zeros(hp - qk)])[None, :]
    sign_row = jnp.concatenate([zeros(rope_lo), -jnp.ones((half,), F32), jnp.ones((half,), F32),
                                zeros(hp - qk)])[None, :]
    q, k, v = _mla_proj(cin, positions[:, :, None], freq_row, sign_row, q_norm[None, :], kv_norm[None, :],
                        w_uq_pad, w_uk_pad, w_uv, tm=_tile(s, 512))
    attn = _attention(q, k, v, tq=_tile(s, 512))
    return _mm_res_ln(attn, w_out.astype(BF16), x, gate, lng, lnb, tm=_tile(s, 512), name="mla_out_proj")


def _sg_layer(x, sc, sh, gate, lng, lnb, w_in, b_in, ln_g, ln_b, w_s, b_s, w_out):
    s = x.shape[1]
    uv = _sg_in_proj(x, sc, sh, w_in.astype(BF16), b_in[None, :], ln_g[None, :], ln_b[None, :], tm=_tile(s, 512))
    return _sg_out(uv, w_s, b_s.T, w_out.astype(BF16), x, gate, lng, lnb, tm=_tile(s, 512))


def _dense_ffn(x, sc, sh, gate, lng, lnb, w_gate, w_up, w_down):
    s = x.shape[1]
    return _swiglu_res_ln(x, sc, sh, w_gate.astype(BF16), w_up.astype(BF16), w_down.astype(BF16),
                          gate, lng, lnb, tm=_tile(s, 512))


def _moe_ffn(x, sc, sh, gate, lng, lnb, w_router, w_gate_all, w_up_all, w_down_all, layer):
    b, s, d = x.shape
    ne = w_router.shape[1]
    ff = w_gate_all.shape[-1]
    bi = _tile(s, MOE_BLOCK)
    nblk = b * (s // bi)
    tokens = b * s
    c = MOE_COMBINE_TILES
    n_max = 2 * tokens // MOE_ROWS + nblk * ne
    n2_max = (n_max + ne * (MOE_GROUP - 1) + MOE_GROUP - 1) // MOE_GROUP
    n_bundles_max = (n_max + (c - 1) * nblk + c - 1) // c
    w_router_pad = jnp.pad(w_router, ((0, 0), (0, LANES - ne)))
    h, comb, pos, post, cnt = _router(x, sc, sh, w_router_pad, tm=bi)
    tiles, (exp2, tile2, n2), bundles = _moe_tile_lists(
        cnt[:, 0, :ne].astype(jnp.int32), n_max=n_max, n2_max=n2_max, n_bundles_max=n_bundles_max,
        rows=MOE_ROWS, group=MOE_GROUP)
    xg = _moe_dispatch(tiles, h.reshape(tokens, d), post, bi=bi, rows=MOE_ROWS, n_max=n_max,
                       n_rows=n2_max * MOE_GROUP * MOE_ROWS)
    y = _moe_experts(exp2 + layer * ne, tile2, n2, xg,
                     w_gate_all.astype(BF16).reshape(-1, d, ff), w_up_all.astype(BF16).reshape(-1, d, ff),
                     w_down_all.astype(BF16).reshape(-1, ff, d), rows2=MOE_GROUP * MOE_ROWS)
    out = _moe_combine(bundles, y, pos.reshape(tokens, LANES), comb.reshape(tokens, LANES), x.reshape(tokens, d),
                       gate, lng, lnb, bi=bi, rows=MOE_ROWS, blocks_per_batch=s // bi)
    return out.reshape(b, s, d)


def kernel(x, c, positions, ada_w, ada_b, ln_g, ln_b, ssd_w_in, ssd_conv_w, ssd_conv_b, ssd_dt_bias, ssd_a_log, ssd_d_skip, ssd_norm_w, ssd_w_out, mla_w_in, mla_q_norm, mla_kv_norm, mla_w_uq, mla_w_ukv, mla_w_out, sg_w_in, sg_b_in, sg_ln_g, sg_ln_b, sg_w_s, sg_b_s, sg_w_out, ffn_w_gate, ffn_w_up, ffn_w_down, moe_w_router, moe_w_gate, moe_w_up, moe_w_down):
    batch = x.shape[0]
    depth = ada_w.shape[0]
    c_pad = jnp.pad(c, ((0, SUBLANES - batch), (0, 0)))
    mod = _ada_mod(c_pad, ada_w, ada_b)[:, :, :batch]
    for i in range(depth):
        sh_m, sc_m, g_m, sh_f, sc_f, g_f = [mod[i, t][:, None, :] for t in range(6)]
        lng_m, lnb_m = ln_g[i, 0][None, :], ln_b[i, 0][None, :]
        lng_f, lnb_f = ln_g[i, 1][None, :], ln_b[i, 1][None, :]
        kind, j = i % 3, i // 3
        if kind == 0:
            x = _ssd_layer(x, sc_m, sh_m, g_m, lng_m, lnb_m, ssd_w_in[j], ssd_conv_w[j], ssd_conv_b[j],
                           ssd_dt_bias[j], ssd_a_log[j], ssd_d_skip[j], ssd_norm_w[j], ssd_w_out[j])
        elif kind == 1:
            x = _mla_layer(x, positions, sc_m, sh_m, g_m, lng_m, lnb_m, mla_w_in[j], mla_q_norm[j],
                           mla_kv_norm[j], mla_w_uq[j], mla_w_ukv[j], mla_w_out[j])
        else:
            x = _sg_layer(x, sc_m, sh_m, g_m, lng_m, lnb_m, sg_w_in[j], sg_b_in[j], sg_ln_g[j], sg_ln_b[j],
                          sg_w_s[j], sg_b_s[j], sg_w_out[j])
        k = i // 2
        if i % 2 == 0:
            x = _dense_ffn(x, sc_f, sh_f, g_f, lng_f, lnb_f, ffn_w_gate[k], ffn_w_up[k], ffn_w_down[k])
        else:
            x = _moe_ffn(x, sc_f, sh_f, g_f, lng_f, lnb_f, moe_w_router[k], moe_w_gate, moe_w_up, moe_w_down, k)
    return x
```

```python
import functools
import math

import jax
import jax.numpy as jnp
from jax import lax
from jax.experimental import pallas as pl
from jax.experimental.pallas import tpu as pltpu

F32 = jnp.float32
BF16 = jnp.bfloat16
HIGHEST = lax.Precision.HIGHEST

DEPTH = 4
ALPHA = (2.0 * DEPTH) ** 0.25
LN_EPS = 1e-5
RMS_EPS = 1e-6
ROPE_THETA = 10000.0

LANES = 128
SUBLANES = 8
VMEM_LIMIT = 56 * 1024 * 1024

SSD_HEAD_DIM = 64
SSD_N_GROUPS = 8
SSD_HPG = 4
SSD_D_STATE = 128
SSD_CONV = 4
SSD_CHUNK = 128
SSD_GROUP_W = SSD_HPG * SSD_HEAD_DIM

MLA_N_HEADS = 16
MLA_NOPE = 64
MLA_ROPE = 32
MLA_V = 64
MLA_Q_RANK = 512
MLA_KV_RANK = 256
MLA_HEAD_PAD = 128
MLA_IN_PAD = MLA_Q_RANK + MLA_KV_RANK + MLA_HEAD_PAD
QK_SCALE = (MLA_NOPE + MLA_ROPE) ** -0.5
LOG2E = 1.4426950408889634
NEG_BIG = -1e30

SG_GROUPS = 8
SG_CHUNK = 128

N_EXPERTS = 8
MOE_BLOCK = 1024
MOE_ROWS = 128
MOE_GROUP = 4
MOE_COMBINE_TILES = 4


def _cparams(*sem):
    return pltpu.CompilerParams(dimension_semantics=sem, vmem_limit_bytes=VMEM_LIMIT)


def _layer_norm(r):
    mu = jnp.mean(r, axis=-1, keepdims=True)
    d = r - mu
    var = jnp.mean(d * d, axis=-1, keepdims=True)
    return d * lax.rsqrt(var + LN_EPS)


def _silu(v):
    return v * jax.nn.sigmoid(v)


def _modulate(x_ref, sc_ref, sh_ref):
    return x_ref[0] * (1.0 + sc_ref[0]) + sh_ref[0]


def _res_ln(x_ref, gate_ref, y, lng_ref, lnb_ref):
    r = ALPHA * x_ref[0] + (1.0 + gate_ref[0]) * y
    return _layer_norm(r) * lng_ref[...] + lnb_ref[...]


def _ada_kernel(c_ref, w_ref, b_ref, o_ref):
    cond = _silu(c_ref[...])
    o_ref[0, 0] = jnp.dot(cond, w_ref[0], preferred_element_type=F32, precision=HIGHEST) + b_ref[0, 0]


def _ada_mod(c_pad, ada_w, ada_b):
    depth, d, _ = ada_w.shape
    rows = c_pad.shape[0]
    return pl.pallas_call(
        _ada_kernel,
        grid=(depth, 6),
        in_specs=[
            pl.BlockSpec((rows, d), lambda i, j: (0, 0)),
            pl.BlockSpec((1, d, d), lambda i, j: (i, 0, j)),
            pl.BlockSpec((1, 1, 1, d), lambda i, j: (i, j, 0, 0)),
        ],
        out_specs=pl.BlockSpec((1, 1, rows, d), lambda i, j: (i, j, 0, 0)),
        out_shape=jax.ShapeDtypeStruct((depth, 6, rows, d), F32),
        compiler_params=_cparams("arbitrary", "arbitrary"),
        name="ada_mod",
    )(c_pad, ada_w, ada_b.reshape(depth, 6, 1, d))


def _modmm_kernel(x_ref, sc_ref, sh_ref, w_ref, o_ref, h_ref):
    @pl.when(pl.program_id(2) == 0)
    def _():
        h_ref[...] = _modulate(x_ref, sc_ref, sh_ref).astype(BF16)

    o_ref[0] = jnp.dot(h_ref[...], w_ref[...], preferred_element_type=F32).astype(o_ref.dtype)


def _mod_matmul(x, sc, sh, w, *, tm, tn, out_dtype, name):
    b, s, d = x.shape
    n = w.shape[1]
    return pl.pallas_call(
        _modmm_kernel,
        grid=(b, s // tm, n // tn),
        in_specs=[
            pl.BlockSpec((1, tm, d), lambda bi, i, j: (bi, i, 0)),
            pl.BlockSpec((1, 1, d), lambda bi, i, j: (bi, 0, 0)),
            pl.BlockSpec((1, 1, d), lambda bi, i, j: (bi, 0, 0)),
            pl.BlockSpec((d, tn), lambda bi, i, j: (0, j)),
        ],
        out_specs=pl.BlockSpec((1, tm, tn), lambda bi, i, j: (bi, i, j)),
        out_shape=jax.ShapeDtypeStruct((b, s, n), out_dtype),
        scratch_shapes=[pltpu.VMEM((tm, d), BF16)],
        compiler_params=_cparams("parallel", "parallel", "arbitrary"),
        name=name,
    )(x, sc, sh, w)


def _ssd_in_kernel(x_ref, sc_ref, sh_ref, w_ref, wdt_ref, o_ref, dt_ref, h_ref):
    @pl.when(pl.program_id(2) == 0)
    def _():
        h_ref[...] = _modulate(x_ref, sc_ref, sh_ref).astype(BF16)
        dt_ref[0] = jnp.dot(h_ref[...], wdt_ref[...], preferred_element_type=F32)

    o_ref[0] = jnp.dot(h_ref[...], w_ref[...], preferred_element_type=F32).astype(o_ref.dtype)


def _ssd_in_proj(x, sc, sh, w_zxbc, w_dt, *, tm, tn):
    b, s, d = x.shape
    n = w_zxbc.shape[1]
    return pl.pallas_call(
        _ssd_in_kernel,
        grid=(b, s // tm, n // tn),
        in_specs=[
            pl.BlockSpec((1, tm, d), lambda bi, i, j: (bi, i, 0)),
            pl.BlockSpec((1, 1, d), lambda bi, i, j: (bi, 0, 0)),
            pl.BlockSpec((1, 1, d), lambda bi, i, j: (bi, 0, 0)),
            pl.BlockSpec((d, tn), lambda bi, i, j: (0, j)),
            pl.BlockSpec((d, LANES), lambda bi, i, j: (0, 0)),
        ],
        out_specs=[
            pl.BlockSpec((1, tm, tn), lambda bi, i, j: (bi, i, j)),
            pl.BlockSpec((1, tm, LANES), lambda bi, i, j: (bi, i, 0)),
        ],
        out_shape=[
            jax.ShapeDtypeStruct((b, s, n), BF16),
            jax.ShapeDtypeStruct((b, s, LANES), F32),
        ],
        scratch_shapes=[pltpu.VMEM((tm, d), BF16)],
        compiler_params=_cparams("parallel", "parallel", "arbitrary"),
        name="ssd_in_proj",
    )(x, sc, sh, w_zxbc, w_dt)


def _ssd_kernel(z_ref, x_ref, bc_ref, dt_ref, cwx_ref, cwbc_ref, cbx_ref, cbbc_ref, dtb_ref, alog_ref,
                dskip_ref, nw_ref, o_ref, prevx_ref, prevbc_ref, taps_ref, xs_ref, b_ref, c_ref, xh_ref,
                acum_ref, key_ref, wend_ref, state_ref):
    L = SSD_CHUNK
    gw = SSD_GROUP_W
    n_st = SSD_D_STATE
    slab = 2 * SUBLANES
    ci = pl.program_id(1)

    @pl.when(ci == 0)
    def _():
        prevx_ref[...] = jnp.zeros(prevx_ref.shape, BF16)
        prevbc_ref[...] = jnp.zeros(prevbc_ref.shape, BF16)
        state_ref[...] = jnp.zeros(state_ref.shape, F32)

    t_idx = lax.broadcasted_iota(jnp.int32, (L, SSD_CONV * L), 0)
    c_idx = lax.broadcasted_iota(jnp.int32, (L, SSD_CONV * L), 1)
    shift = (SSD_CONV - 1) - c_idx // L
    delta = t_idx - c_idx % L
    rot = jnp.where((delta == shift) | (delta == shift - L), 1.0, 0.0).astype(BF16)
    slab_row = lax.broadcasted_iota(jnp.int32, (slab, x_ref.shape[2]), 0)

    def conv(cur_ref, prev_ref, cw_ref, cb_ref):
        body = cur_ref[0, 0:L - slab, :]
        last = cur_ref[0, L - slab:L, :]
        prev = prev_ref[...]
        for k in range(SSD_CONV):
            w_k = cw_ref[k:k + 1, :].astype(BF16)
            swapped = jnp.where(slab_row >= slab - (SSD_CONV - 1 - k), prev, last)
            taps_ref[k * L:(k + 1) * L - slab, :] = body * w_k
            taps_ref[(k + 1) * L - slab:(k + 1) * L, :] = swapped * w_k
        prev_ref[...] = last
        return _silu(jnp.dot(rot, taps_ref[...], preferred_element_type=F32) + cb_ref[...])

    xs = conv(x_ref, prevx_ref, cwx_ref, cbx_ref)
    xs_ref[...] = xs
    xh_ref[...] = xs.astype(BF16)
    bc = conv(bc_ref, prevbc_ref, cwbc_ref, cbbc_ref)
    half = bc.shape[1] // 2
    b_ref[...] = bc[:, :half]
    c_ref[...] = bc[:, half:].astype(BF16)

    dtv = dt_ref[0] + dtb_ref[...]
    dt = jnp.maximum(dtv, 0.0) + jnp.log1p(jnp.exp(-jnp.abs(dtv)))
    a_neg = -jnp.exp(alog_ref[...])
    row = lax.broadcasted_iota(jnp.int32, (L, L), 0)
    col = lax.broadcasted_iota(jnp.int32, (L, L), 1)
    causal = col <= row
    tri = jnp.where(causal, 1.0, 0.0).astype(F32)
    a_cum = jnp.dot(tri, dt * (a_neg * LOG2E), preferred_element_type=F32, precision=HIGHEST)
    a_cum_t = a_cum.T
    key_t = a_cum_t - jnp.log2(dt.T)
    w_end_t = jnp.exp2(a_cum_t[:, L - 1:L] - key_t)

    acum_ref[...] = a_cum
    key_ref[...] = key_t
    wend_ref[...] = w_end_t

    def group(g):
        lane = lax.broadcasted_iota(jnp.int32, (L, LANES), 1)
        head_of_lane = lax.broadcasted_iota(jnp.int32, (L, gw), 1) // SSD_HEAD_DIM
        causal = lax.broadcasted_iota(jnp.int32, (L, L), 1) <= lax.broadcasted_iota(jnp.int32, (L, L), 0)
        sl = slice(g * gw, (g + 1) * gw)
        sn = slice(g * n_st, (g + 1) * n_st)
        c_g = c_ref[:, sn]
        b_g32 = b_ref[:, sn]
        b_gt = b_g32.T
        cb = lax.dot_general(c_g, b_g32.astype(BF16), (((1,), (1,)), ((), ())), preferred_element_type=F32)
        st = state_ref[g]
        y_off = jnp.dot(c_g, st.astype(BF16), preferred_element_type=F32)
        xh_g = xh_ref[:, sl]
        m_heads, bw_heads, x_heads, ea_heads = [], [], [], []
        for r in range(SSD_HPG):
            h = SSD_HPG * g + r
            a_col = jnp.broadcast_to(acum_ref[:, h:h + 1], (L, L))
            decay_dt = jnp.exp2(jnp.where(causal, a_col - key_ref[h:h + 1, :], NEG_BIG))
            m_heads.append((cb * decay_dt).astype(BF16))
            bw_heads.append((b_gt * wend_ref[h:h + 1, :]).astype(BF16))
            x_heads.append(jnp.where(head_of_lane == r, xh_g, jnp.zeros_like(xh_g)))
            ea_heads.append(jnp.exp2(a_col))
        x_stack = jnp.concatenate(x_heads, axis=0)
        y_diag = jnp.dot(jnp.concatenate(m_heads, axis=1), x_stack, preferred_element_type=F32)
        new_st = jnp.dot(jnp.concatenate(bw_heads, axis=1), x_stack, preferred_element_type=F32)
        ea_g = jnp.concatenate([jnp.where(lane < SSD_HEAD_DIM, ea_heads[0], ea_heads[1]),
                                jnp.where(lane < SSD_HEAD_DIM, ea_heads[2], ea_heads[3])], axis=1)
        y = y_diag + y_off * ea_g + xs_ref[:, sl] * dskip_ref[:, sl]
        z_g = z_ref[0, :, sl].astype(F32)
        t = y * _silu(z_g)
        ms = jnp.mean(t * t, axis=-1, keepdims=True)
        o_ref[0, :, sl] = (t * lax.rsqrt(ms + RMS_EPS) * nw_ref[:, sl]).astype(o_ref.dtype)
        state_ref[g] = st * ea_g[L - 1:L, :] + new_st

    for g in range(SSD_N_GROUPS):
        group(g)


def _ssd_scan(zxbc, dt_raw, conv_w, conv_b, dt_bias, a_log, d_skip, norm_w):
    b, s, _ = zxbc.shape
    L = SSD_CHUNK
    di = SSD_N_GROUPS * SSD_GROUP_W
    blk = lambda j: pl.BlockSpec((1, L, di), lambda bi, ci, j=j: (bi, ci, j))
    vec = lambda w: pl.BlockSpec((1, w), lambda bi, ci: (0, 0))
    cw = lambda j: pl.BlockSpec((SSD_CONV, di), lambda bi, ci, j=j: (0, j))
    cb = lambda j: pl.BlockSpec((1, di), lambda bi, ci, j=j: (0, j))
    return pl.pallas_call(
        _ssd_kernel,
        grid=(b, s // L),
        in_specs=[
            blk(0), blk(1), blk(2),
            pl.BlockSpec((1, L, LANES), lambda bi, ci: (bi, ci, 0)),
            cw(0), cw(1), cb(0), cb(1),
            vec(LANES), vec(LANES), vec(di), vec(di),
        ],
        out_specs=pl.BlockSpec((1, L, di), lambda bi, ci: (bi, ci, 0)),
        out_shape=jax.ShapeDtypeStruct((b, s, di), BF16),
        scratch_shapes=[
            pltpu.VMEM((2 * SUBLANES, di), BF16),
            pltpu.VMEM((2 * SUBLANES, di), BF16),
            pltpu.VMEM((SSD_CONV * L, di), BF16),
            pltpu.VMEM((L, di), F32),
            pltpu.VMEM((L, di // 2), F32),
            pltpu.VMEM((L, di // 2), BF16),
            pltpu.VMEM((L, di), BF16),
            pltpu.VMEM((L, LANES), F32),
            pltpu.VMEM((LANES, L), F32),
            pltpu.VMEM((LANES, L), F32),
            pltpu.VMEM((SSD_N_GROUPS, SSD_D_STATE, SSD_GROUP_W), F32),
        ],
        compiler_params=_cparams("arbitrary", "arbitrary"),
        name="ssd_scan",
    )(zxbc, zxbc, zxbc, dt_raw, conv_w, conv_w, conv_b, conv_b, dt_bias, a_log, d_skip, norm_w)


def _mm_res_ln_kernel(a_ref, w_ref, x_ref, gate_ref, lng_ref, lnb_ref, o_ref):
    y = jnp.dot(a_ref[0], w_ref[...], preferred_element_type=F32)
    o_ref[0] = _res_ln(x_ref, gate_ref, y, lng_ref, lnb_ref)


def _mm_res_ln(a, w, x, gate, lng, lnb, *, tm, name):
    b, s, d = x.shape
    k = a.shape[2]
    return pl.pallas_call(
        _mm_res_ln_kernel,
        grid=(b, s // tm),
        in_specs=[
            pl.BlockSpec((1, tm, k), lambda bi, i: (bi, i, 0)),
            pl.BlockSpec((k, d), lambda bi, i: (0, 0)),
            pl.BlockSpec((1, tm, d), lambda bi, i: (bi, i, 0)),
            pl.BlockSpec((1, 1, d), lambda bi, i: (bi, 0, 0)),
            pl.BlockSpec((1, d), lambda bi, i: (0, 0)),
            pl.BlockSpec((1, d), lambda bi, i: (0, 0)),
        ],
        out_specs=pl.BlockSpec((1, tm, d), lambda bi, i: (bi, i, 0)),
        out_shape=jax.ShapeDtypeStruct((b, s, d), F32),
        compiler_params=_cparams("parallel", "parallel"),
        name=name,
    )(a, w, x, gate, lng, lnb)


def _mla_proj_kernel(cin_ref, pos_ref, freq_ref, sign_ref, qn_ref, kvn_ref, wuq_ref, wuk_ref, wuv_ref,
                     q_ref, k_ref, v_ref):
    cin = cin_ref[0]
    tm = cin.shape[0]
    hp = MLA_HEAD_PAD

    def rms(v, w_ref):
        return v * lax.rsqrt(jnp.mean(v * v, axis=-1, keepdims=True) + RMS_EPS) * w_ref[...]

    cq = rms(cin[:, :MLA_Q_RANK], qn_ref).astype(BF16)
    ckv = rms(cin[:, MLA_Q_RANK:MLA_Q_RANK + MLA_KV_RANK], kvn_ref).astype(BF16)
    k_rope = cin[:, MLA_Q_RANK + MLA_KV_RANK:]
    q = jnp.dot(cq, wuq_ref[...], preferred_element_type=F32)
    k = jnp.dot(ckv, wuk_ref[...], preferred_element_type=F32)
    ones_col = jnp.where(lax.broadcasted_iota(jnp.int32, (1, wuv_ref.shape[1]), 1) % hp == MLA_V, 1.0, 0.0)
    v_ref[0] = (jnp.dot(ckv, wuv_ref[...], preferred_element_type=F32) + ones_col).astype(v_ref.dtype)

    ang = pos_ref[0].astype(F32) * freq_ref[...]
    cos = jnp.cos(ang)
    sin = jnp.sin(ang) * sign_ref[...]
    lane = lax.broadcasted_iota(jnp.int32, (tm, hp), 1)
    first_half = lane < MLA_NOPE + MLA_ROPE // 2

    def rope(xh):
        swapped = jnp.where(first_half,
                            pltpu.roll(xh, hp - MLA_ROPE // 2, 1),
                            pltpu.roll(xh, MLA_ROPE // 2, 1))
        return xh * cos + swapped * sin

    kr = rope(k_rope)
    for h in range(MLA_N_HEADS):
        sl = slice(h * hp, (h + 1) * hp)
        q_ref[0, :, sl] = (rope(q[:, sl]) * (QK_SCALE * LOG2E)).astype(q_ref.dtype)
        k_ref[0, :, sl] = (k[:, sl] + kr).astype(k_ref.dtype)


def _mla_proj(cin, pos, freq, sign, q_norm, kv_norm, w_uq, w_uk, w_uv, *, tm):
    b, s, n_in = cin.shape
    nq = w_uq.shape[1]
    nv = w_uv.shape[1]
    full = lambda a: pl.BlockSpec(a.shape, lambda bi, i: (0,) * a.ndim)
    return pl.pallas_call(
        _mla_proj_kernel,
        grid=(b, s // tm),
        in_specs=[
            pl.BlockSpec((1, tm, n_in), lambda bi, i: (bi, i, 0)),
            pl.BlockSpec((1, tm, 1), lambda bi, i: (bi, i, 0)),
            full(freq), full(sign), full(q_norm), full(kv_norm), full(w_uq), full(w_uk), full(w_uv),
        ],
        out_specs=[
            pl.BlockSpec((1, tm, nq), lambda bi, i: (bi, i, 0)),
            pl.BlockSpec((1, tm, nq), lambda bi, i: (bi, i, 0)),
            pl.BlockSpec((1, tm, nv), lambda bi, i: (bi, i, 0)),
        ],
        out_shape=[
            jax.ShapeDtypeStruct((b, s, nq), BF16),
            jax.ShapeDtypeStruct((b, s, nq), BF16),
            jax.ShapeDtypeStruct((b, s, nv), BF16),
        ],
        compiler_params=_cparams("parallel", "parallel"),
        name="mla_proj",
    )(cin, pos, freq, sign, q_norm, kv_norm, w_uq, w_uk, w_uv)


def _attn_kernel(q_ref, k_ref, v_ref, o_ref, acc_ref, s0_ref, s1_ref, *, tq, n_kb):
    qi = pl.program_id(2)
    hp = MLA_HEAD_PAD
    def scores(kb, s_ref, masked=True):
        start = pl.multiple_of(jnp.minimum(kb, n_kb - 1) * tq, tq)
        if masked:
            key_minus_query = (lax.broadcasted_iota(jnp.int32, (tq, tq), 1)
                               - lax.broadcasted_iota(jnp.int32, (tq, tq), 0))
            keep = key_minus_query <= (qi - kb) * tq
        for hh in range(2):
            q = q_ref[0, :, hh * hp:(hh + 1) * hp]
            k = k_ref[0, pl.ds(start, tq), hh * hp:(hh + 1) * hp]
            s = lax.dot_general(q, k, (((1,), (1,)), ((), ())), preferred_element_type=F32)
            s_ref[hh] = jnp.where(keep, s, NEG_BIG) if masked else s

    def softmax_pv(kb, s_ref, m_prev):
        start = pl.multiple_of(jnp.minimum(kb, n_kb - 1) * tq, tq)
        m_out = []
        for hh in range(2):
            v = v_ref[0, pl.ds(start, tq), hh * hp:(hh + 1) * hp]
            s = s_ref[hh]
            m_new = jnp.maximum(m_prev[hh], jnp.max(s, axis=1, keepdims=True))
            alpha = jnp.exp2(m_prev[hh] - m_new)
            p = jnp.exp2((s - m_new).astype(BF16))
            acc_ref[hh] = acc_ref[hh] * alpha + jnp.dot(p, v, preferred_element_type=F32)
            m_out.append(m_new)
        return tuple(m_out)

    acc_ref[...] = jnp.zeros(acc_ref.shape, F32)
    m_init = jnp.full((tq, 1), NEG_BIG, F32)
    scores(0, s0_ref)

    def two_blocks(t, m, mask_ahead):
        m = softmax_pv(2 * t, s0_ref, m)
        scores(2 * t + 1, s1_ref, masked=False)
        m = softmax_pv(2 * t + 1, s1_ref, m)
        scores(2 * t + 2, s0_ref, masked=mask_ahead)
        return m

    n_pairs = (qi + 2) // 2
    n_plain = jnp.maximum(n_pairs - 2, 0)
    m = lax.fori_loop(0, n_plain, functools.partial(two_blocks, mask_ahead=False), (m_init, m_init))
    m = lax.fori_loop(n_plain, n_pairs - 1, functools.partial(two_blocks, mask_ahead=True), m)
    m = softmax_pv(2 * n_pairs - 2, s0_ref, m)

    def second_block(_, m):
        scores(2 * n_pairs - 1, s1_ref)
        return softmax_pv(2 * n_pairs - 1, s1_ref, m)

    lax.fori_loop(0, qi % 2, second_block, m)
    out_a = acc_ref[0] / acc_ref[0][:, MLA_V:MLA_V + 1]
    out_b = acc_ref[1] / acc_ref[1][:, MLA_V:MLA_V + 1]
    lane = lax.broadcasted_iota(jnp.int32, (tq, hp), 1)
    o_ref[0] = jnp.where(lane < MLA_V, out_a, pltpu.roll(out_b, MLA_V, 1)).astype(o_ref.dtype)


def _attention(q, k, v, *, tq):
    b, s, _ = q.shape
    pairs = MLA_N_HEADS // 2
    return pl.pallas_call(
        functools.partial(_attn_kernel, tq=tq, n_kb=s // tq),
        grid=(b, pairs, s // tq),
        in_specs=[
            pl.BlockSpec((1, tq, 2 * MLA_HEAD_PAD), lambda bi, j, i: (bi, i, j)),
            pl.BlockSpec((1, s, 2 * MLA_HEAD_PAD), lambda bi, j, i: (bi, 0, j)),
            pl.BlockSpec((1, s, 2 * MLA_HEAD_PAD), lambda bi, j, i: (bi, 0, j)),
        ],
        out_specs=pl.BlockSpec((1, tq, 2 * MLA_V), lambda bi, j, i: (bi, i, j)),
        out_shape=jax.ShapeDtypeStruct((b, s, MLA_N_HEADS * MLA_V), BF16),
        scratch_shapes=[pltpu.VMEM((2, tq, MLA_HEAD_PAD), F32), pltpu.VMEM((2, tq, tq), F32),
                        pltpu.VMEM((2, tq, tq), F32)],
        compiler_params=_cparams("parallel", "parallel", "arbitrary"),
        name="mla_attention",
    )(q, k, v)


def _sg_in_kernel(x_ref, sc_ref, sh_ref, w_ref, b_ref, lng_ref, lnb_ref, o_ref, h_ref):
    j = pl.program_id(2)

    @pl.when(j == 0)
    def _():
        h_ref[...] = _modulate(x_ref, sc_ref, sh_ref).astype(BF16)

    pre = jnp.dot(h_ref[...], w_ref[...], preferred_element_type=F32) + b_ref[...]
    y = jax.nn.gelu(pre.astype(BF16))

    @pl.when(j == 0)
    def _():
        o_ref[0] = y.astype(o_ref.dtype)

    @pl.when(j == 1)
    def _():
        o_ref[0] = (_layer_norm(y.astype(F32)) * lng_ref[...] + lnb_ref[...]).astype(o_ref.dtype)


def _sg_in_proj(x, sc, sh, w, bias, lng, lnb, *, tm):
    b, s, d = x.shape
    n = w.shape[1]
    tn = n // 2
    return pl.pallas_call(
        _sg_in_kernel,
        grid=(b, s // tm, 2),
        in_specs=[
            pl.BlockSpec((1, tm, d), lambda bi, i, j: (bi, i, 0)),
            pl.BlockSpec((1, 1, d), lambda bi, i, j: (bi, 0, 0)),
            pl.BlockSpec((1, 1, d), lambda bi, i, j: (bi, 0, 0)),
            pl.BlockSpec((d, tn), lambda bi, i, j: (0, j)),
            pl.BlockSpec((1, tn), lambda bi, i, j: (0, j)),
            pl.BlockSpec((1, tn), lambda bi, i, j: (0, 0)),
            pl.BlockSpec((1, tn), lambda bi, i, j: (0, 0)),
        ],
        out_specs=pl.BlockSpec((1, tm, tn), lambda bi, i, j: (bi, i, j)),
        out_shape=jax.ShapeDtypeStruct((b, s, n), BF16),
        scratch_shapes=[pltpu.VMEM((tm, d), BF16)],
        compiler_params=_cparams("parallel", "parallel", "arbitrary"),
        name="sg_in_proj",
    )(x, sc, sh, w, bias, lng, lnb)


def _sg_out_kernel(u_ref, v_ref, ws_ref, bst_ref, w_ref, x_ref, gate_ref, lng_ref, lnb_ref, o_ref, gated_ref):
    tm = u_ref.shape[1]
    L = SG_CHUNK
    gd = u_ref.shape[2] // SG_GROUPS
    row = lax.broadcasted_iota(jnp.int32, (L, L), 0)
    col = lax.broadcasted_iota(jnp.int32, (L, L), 1)
    causal = col <= row
    for g in range(SG_GROUPS):
        ws = jnp.where(causal, ws_ref[g], 0.0).astype(BF16)
        bias = bst_ref[:, g:g + 1]
        for c in range(tm // L):
            rs = slice(c * L, (c + 1) * L)
            cs = slice(g * gd, (g + 1) * gd)
            mixed = jnp.dot(ws, v_ref[0, rs, cs], preferred_element_type=F32) + bias
            gated_ref[rs, cs] = (u_ref[0, rs, cs].astype(F32) * mixed).astype(BF16)
    y = jnp.dot(gated_ref[...], w_ref[...], preferred_element_type=F32)
    o_ref[0] = _res_ln(x_ref, gate_ref, y, lng_ref, lnb_ref)


def _sg_out(uv, w_s, b_s_t, w_out, x, gate, lng, lnb, *, tm):
    b, s, d = x.shape
    sgd = uv.shape[2] // 2
    full = lambda a: pl.BlockSpec(a.shape, lambda bi, i: (0,) * a.ndim)
    return pl.pallas_call(
        _sg_out_kernel,
        grid=(b, s // tm),
        in_specs=[
            pl.BlockSpec((1, tm, sgd), lambda bi, i: (bi, i, 0)),
            pl.BlockSpec((1, tm, sgd), lambda bi, i: (bi, i, 1)),
            full(w_s), full(b_s_t), full(w_out),
            pl.BlockSpec((1, tm, d), lambda bi, i: (bi, i, 0)),
            pl.BlockSpec((1, 1, d), lambda bi, i: (bi, 0, 0)),
            pl.BlockSpec((1, d), lambda bi, i: (0, 0)),
            pl.BlockSpec((1, d), lambda bi, i: (0, 0)),
        ],
        out_specs=pl.BlockSpec((1, tm, d), lambda bi, i: (bi, i, 0)),
        out_shape=jax.ShapeDtypeStruct((b, s, d), F32),
        scratch_shapes=[pltpu.VMEM((tm, sgd), BF16)],
        compiler_params=_cparams("parallel", "parallel"),
        name="sg_out",
    )(uv, uv, w_s, b_s_t, w_out, x, gate, lng, lnb)


def _swiglu_kernel(x_ref, sc_ref, sh_ref, wg_ref, wu_ref, wd_ref, gate_ref, lng_ref, lnb_ref, o_ref, *, n_split):
    h = _modulate(x_ref, sc_ref, sh_ref).astype(BF16)
    tf = wg_ref.shape[1] // n_split
    acc = jnp.zeros(o_ref.shape[1:], F32)
    for f in range(n_split):
        fs = slice(f * tf, (f + 1) * tf)
        g = jnp.dot(h, wg_ref[:, fs], preferred_element_type=F32)
        u = jnp.dot(h, wu_ref[:, fs], preferred_element_type=F32)
        acc = acc + jnp.dot((_silu(g) * u).astype(BF16), wd_ref[fs, :], preferred_element_type=F32)
    o_ref[0] = _res_ln(x_ref, gate_ref, acc, lng_ref, lnb_ref)


def _swiglu_res_ln(x, sc, sh, wg, wu, wd, gate, lng, lnb, *, tm):
    b, s, d = x.shape
    whole = lambda a: pl.BlockSpec(a.shape, lambda bi, i: (0, 0))
    return pl.pallas_call(
        functools.partial(_swiglu_kernel, n_split=2),
        grid=(b, s // tm),
        in_specs=[
            pl.BlockSpec((1, tm, d), lambda bi, i: (bi, i, 0)),
            pl.BlockSpec((1, 1, d), lambda bi, i: (bi, 0, 0)),
            pl.BlockSpec((1, 1, d), lambda bi, i: (bi, 0, 0)),
            whole(wg), whole(wu), whole(wd),
            pl.BlockSpec((1, 1, d), lambda bi, i: (bi, 0, 0)),
            pl.BlockSpec((1, d), lambda bi, i: (0, 0)),
            pl.BlockSpec((1, d), lambda bi, i: (0, 0)),
        ],
        out_specs=pl.BlockSpec((1, tm, d), lambda bi, i: (bi, i, 0)),
        out_shape=jax.ShapeDtypeStruct((b, s, d), F32),
        compiler_params=_cparams("parallel", "parallel"),
        name="swiglu_res_ln",
    )(x, sc, sh, wg, wu, wd, gate, lng, lnb)


def _router_kernel(x_ref, sc_ref, sh_ref, wr_ref, h_ref, comb_ref, pos_ref, post_ref, cnt_ref):
    h = _modulate(x_ref, sc_ref, sh_ref)
    h_ref[0] = h.astype(h_ref.dtype)
    logits = jnp.dot(h, wr_ref[...], preferred_element_type=F32, precision=HIGHEST)
    tm = logits.shape[0]
    lane = lax.broadcasted_iota(jnp.int32, logits.shape, 1).astype(F32)
    lg = jnp.where(lane < N_EXPERTS, logits, -jnp.inf)
    m1 = jnp.max(lg, axis=1, keepdims=True)
    i1 = jnp.min(jnp.where(lg == m1, lane, float(LANES)), axis=1, keepdims=True)
    lg2 = jnp.where(lane == i1, -jnp.inf, lg)
    m2 = jnp.max(lg2, axis=1, keepdims=True)
    i2 = jnp.min(jnp.where(lg2 == m2, lane, float(LANES)), axis=1, keepdims=True)
    e2 = jnp.exp(m2 - m1)
    den = 1.0 + e2
    comb_ref[0] = jnp.where(lane == i1, 1.0 / den, 0.0) + jnp.where(lane == i2, e2 / den, 0.0)
    sel = (lane == i1) | (lane == i2)
    earlier = (lax.broadcasted_iota(jnp.int32, (tm, tm), 1) < lax.broadcasted_iota(jnp.int32, (tm, tm), 0))
    sel_f = jnp.where(sel, 1.0, 0.0)
    rank = jnp.dot(jnp.where(earlier, 1.0, 0.0).astype(BF16), sel_f.astype(BF16), preferred_element_type=F32)
    pos = jnp.where(sel, rank, -1.0)
    pos_ref[0] = pos
    post_ref[0] = pos.T[0:SUBLANES, :]
    cnt_ref[0] = jnp.sum(sel_f, axis=0, keepdims=True)


def _router(x, sc, sh, w_router_pad, *, tm):
    b, s, d = x.shape
    nbs = s // tm
    tok = lambda w: pl.BlockSpec((1, tm, w), lambda bi, i: (bi, i, 0))
    return pl.pallas_call(
        _router_kernel,
        grid=(b, nbs),
        in_specs=[
            tok(d),
            pl.BlockSpec((1, 1, d), lambda bi, i: (bi, 0, 0)),
            pl.BlockSpec((1, 1, d), lambda bi, i: (bi, 0, 0)),
            pl.BlockSpec((d, LANES), lambda bi, i: (0, 0)),
        ],
        out_specs=[
            tok(d), tok(LANES), tok(LANES),
            pl.BlockSpec((1, SUBLANES, tm), lambda bi, i: (bi * nbs + i, 0, 0)),
            pl.BlockSpec((1, 1, LANES), lambda bi, i: (bi * nbs + i, 0, 0)),
        ],
        out_shape=[
            jax.ShapeDtypeStruct((b, s, d), BF16),
            jax.ShapeDtypeStruct((b, s, LANES), F32),
            jax.ShapeDtypeStruct((b, s, LANES), F32),
            jax.ShapeDtypeStruct((b * nbs, SUBLANES, tm), F32),
            jax.ShapeDtypeStruct((b * nbs, 1, LANES), F32),
        ],
        compiler_params=_cparams("parallel", "parallel"),
        name="moe_router",
    )(x, sc, sh, w_router_pad)


def _moe_dispatch_kernel(blk_ref, exp_ref, sub_ref, dst_ref, n_ref, h_ref, post_ref, o_ref, *, n_max):
    s = pl.program_id(0)

    @pl.when((s < n_ref[0]) | ((s >= n_max) & (s - n_max < n_ref[1])))
    def _():
        rows = o_ref.shape[0]
        pos_row = post_ref[0, pl.ds(exp_ref[s], 1), :]
        want = sub_ref[s] * rows + lax.broadcasted_iota(jnp.int32, (rows, 1), 0)
        onehot = jnp.where(pos_row == want.astype(F32), 1.0, 0.0).astype(BF16)
        o_ref[...] = jnp.dot(onehot, h_ref[...], preferred_element_type=F32).astype(o_ref.dtype)


def _moe_dispatch(lists, h2d, post, *, bi, rows, n_max, n_rows):
    blk, exp, sub, dst, n = lists
    d = h2d.shape[1]
    grid_spec = pltpu.PrefetchScalarGridSpec(
        num_scalar_prefetch=5,
        grid=(blk.shape[0],),
        in_specs=[
            pl.BlockSpec((bi, d), lambda s, blk, exp, sub, dst, n: (blk[s], 0)),
            pl.BlockSpec((1, SUBLANES, bi), lambda s, blk, exp, sub, dst, n: (blk[s], 0, 0)),
        ],
        out_specs=pl.BlockSpec((rows, d), lambda s, blk, exp, sub, dst, n: (dst[s], 0)),
    )
    return pl.pallas_call(
        functools.partial(_moe_dispatch_kernel, n_max=n_max),
        grid_spec=grid_spec,
        out_shape=jax.ShapeDtypeStruct((n_rows, d), BF16),
        compiler_params=_cparams("arbitrary"),
        name="moe_dispatch",
    )(blk, exp, sub, dst, n, h2d, post)


def _moe_expert_kernel(exp_ref, tile_ref, n_ref, xg_ref, wg_ref, wu_ref, wd_ref, o_ref, *, n_split):
    del exp_ref, tile_ref
    s = pl.program_id(0)

    @pl.when(s < n_ref[0])
    def _():
        xg = xg_ref[...]
        ff = wg_ref.shape[2]
        tf = ff // n_split
        acc = jnp.zeros(o_ref.shape, F32)
        for f in range(n_split):
            fs = slice(f * tf, (f + 1) * tf)
            g = jnp.dot(xg, wg_ref[0, :, fs], preferred_element_type=F32)
            u = jnp.dot(xg, wu_ref[0, :, fs], preferred_element_type=F32)
            acc = acc + jnp.dot((_silu(g) * u).astype(BF16), wd_ref[0, fs, :], preferred_element_type=F32)
        o_ref[...] = acc.astype(o_ref.dtype)


def _moe_experts(exp2, tile2, n2, xg, wg, wu, wd, *, rows2):
    _, d, ff = wg.shape
    grid_spec = pltpu.PrefetchScalarGridSpec(
        num_scalar_prefetch=3,
        grid=(exp2.shape[0],),
        in_specs=[
            pl.BlockSpec((rows2, d), lambda s, exp, tile, n: (tile[s], 0)),
            pl.BlockSpec((1, d, ff), lambda s, exp, tile, n: (exp[s], 0, 0)),
            pl.BlockSpec((1, d, ff), lambda s, exp, tile, n: (exp[s], 0, 0)),
            pl.BlockSpec((1, ff, d), lambda s, exp, tile, n: (exp[s], 0, 0)),
        ],
        out_specs=pl.BlockSpec((rows2, d), lambda s, exp, tile, n: (tile[s], 0)),
    )
    return pl.pallas_call(
        functools.partial(_moe_expert_kernel, n_split=2),
        grid_spec=grid_spec,
        out_shape=jax.ShapeDtypeStruct(xg.shape, BF16),
        compiler_params=_cparams("arbitrary"),
        name="moe_experts",
    )(exp2, tile2, n2, xg, wg, wu, wd)


def _moe_combine_kernel(*refs):
    c = MOE_COMBINE_TILES
    first_ref, last_ref, n_ref = refs[1:4]
    slot_refs = refs[4:4 + 3 * c]
    y_refs = refs[4 + 3 * c:4 + 4 * c]
    pos_ref, comb_ref, x_ref, gate_ref, lng_ref, lnb_ref, o_ref, acc_ref = refs[4 + 4 * c:]
    s = pl.program_id(0)

    @pl.when(s < n_ref[0])
    def _():
        @pl.when(first_ref[s] == 1)
        def _():
            acc_ref[...] = jnp.zeros(acc_ref.shape, F32)

        rows = y_refs[0].shape[0]
        pos = pos_ref[...]
        comb = comb_ref[...]
        lane = lax.broadcasted_iota(jnp.int32, pos.shape, 1)
        row_id = lax.broadcasted_iota(jnp.int32, (1, rows), 1)

        def scatter(e, sub):
            pos_e = jnp.max(jnp.where(lane == e, pos, -2.0), axis=1, keepdims=True)
            w_e = jnp.sum(jnp.where(lane == e, comb, 0.0), axis=1, keepdims=True)
            return jnp.where(pos_e == (sub * rows + row_id).astype(F32), w_e, 0.0).astype(BF16)

        scat = jnp.concatenate([scatter(slot_refs[3 * i][s], slot_refs[3 * i + 1][s]) for i in range(c)], axis=1)
        y = jnp.concatenate([r[...] for r in y_refs], axis=0)
        acc_ref[...] += jnp.dot(scat, y, preferred_element_type=F32)

        @pl.when(last_ref[s] == 1)
        def _():
            r = ALPHA * x_ref[...] + (1.0 + gate_ref[0]) * acc_ref[...]
            o_ref[...] = _layer_norm(r) * lng_ref[...] + lnb_ref[...]


def _moe_combine(bundles, y, pos2d, comb2d, x2d, gate, lng, lnb, *, bi, rows, blocks_per_batch):
    d = x2d.shape[1]
    c = MOE_COMBINE_TILES
    by_blk = lambda w: pl.BlockSpec((bi, w), lambda s, blk, *_: (blk[s], 0))
    y_spec = lambda i: pl.BlockSpec((rows, d), lambda s, *pf: (pf[4 + 3 * i + 2][s], 0))
    grid_spec = pltpu.PrefetchScalarGridSpec(
        num_scalar_prefetch=len(bundles),
        grid=(bundles[0].shape[0],),
        in_specs=[y_spec(i) for i in range(c)] + [
            by_blk(LANES), by_blk(LANES), by_blk(d),
            pl.BlockSpec((1, 1, d), lambda s, blk, *_: (blk[s] // blocks_per_batch, 0, 0)),
            pl.BlockSpec((1, d), lambda s, *_: (0, 0)),
            pl.BlockSpec((1, d), lambda s, *_: (0, 0)),
        ],
        out_specs=by_blk(d),
        scratch_shapes=[pltpu.VMEM((bi, d), F32)],
    )
    return pl.pallas_call(
        _moe_combine_kernel,
        grid_spec=grid_spec,
        out_shape=jax.ShapeDtypeStruct(x2d.shape, F32),
        compiler_params=_cparams("arbitrary"),
        name="moe_combine",
    )(*bundles, *([y] * c), pos2d, comb2d, x2d, gate, lng, lnb)


def _moe_tile_lists(cnt, *, n_max, n2_max, n_bundles_max, rows, group):
    nblk, ne = cnt.shape
    i32 = jnp.int32
    nt = (cnt + rows - 1) // rows
    nt_e_pad = (nt.sum(axis=0) + group - 1) // group * group
    e_end = jnp.cumsum(nt_e_pad)
    dst0 = ((e_end - nt_e_pad)[None, :] + jnp.cumsum(nt, axis=0) - nt).reshape(-1)
    flat = nt.reshape(-1)
    ends = jnp.cumsum(flat)
    n_tiles = ends[-1]
    count_le = lambda bounds, v: jnp.sum(bounds[None, :] <= v[:, None], axis=1).astype(i32)
    slot = jnp.minimum(jnp.arange(n_max, dtype=i32), n_tiles - 1)
    seg = count_le(ends, slot)
    blk = seg // ne
    exp = seg % ne
    sub = slot - (ends[seg] - flat[seg])
    dst = dst0[seg] + sub
    no_match = jnp.int32(1 << 20)
    pad_e = nt_e_pad - nt.sum(axis=0)
    pad_end = jnp.cumsum(pad_e)
    n_pad = pad_end[-1]
    q = jnp.minimum(jnp.arange(ne * (group - 1), dtype=i32), jnp.maximum(n_pad - 1, 0))
    q_e = jnp.minimum(count_le(pad_end, q), ne - 1)
    pad_dst = e_end[q_e] - pad_end[q_e] + q
    pad_dst = jnp.where(n_pad > 0, pad_dst, dst[-1])
    cat = lambda a, b: jnp.concatenate([a.astype(i32), b.astype(i32)])
    tiles = (cat(blk, jnp.broadcast_to(blk[-1], q.shape)), cat(exp, jnp.zeros_like(q)),
             cat(sub, jnp.broadcast_to(no_match, q.shape)), cat(dst, pad_dst),
             jnp.stack([n_tiles, n_pad]).astype(i32))
    n2 = e_end[-1] // group
    tile2 = jnp.minimum(jnp.arange(n2_max, dtype=i32), n2 - 1)
    exp2 = jnp.minimum(count_le(e_end, tile2 * group), ne - 1)
    experts = (exp2.astype(i32), tile2.astype(i32), n2.astype(i32)[None])
    c = MOE_COMBINE_TILES
    nt_blk = nt.sum(axis=1)
    blk_end = ends.reshape(nblk, ne)[:, -1]
    blk_begin = blk_end - nt_blk
    nb_blk = (nt_blk + c - 1) // c
    b_end = jnp.cumsum(nb_blk)
    n_bundles = b_end[-1]
    bslot = jnp.minimum(jnp.arange(n_bundles_max, dtype=i32), n_bundles - 1)
    bblk = count_le(b_end, bslot)
    idx = bslot - (b_end[bblk] - nb_blk[bblk])
    t0 = blk_begin[bblk] + c * idx
    no_match = jnp.int32(1 << 20)
    bundles = [bblk.astype(i32), (idx == 0).astype(i32), (idx == nb_blk[bblk] - 1).astype(i32),
               n_bundles.astype(i32)[None]]
    for i in range(c):
        real = t0 + i < blk_end[bblk]
        t = jnp.where(real, t0 + i, t0)
        bundles += [exp[t].astype(i32), jnp.where(real, sub[t], no_match).astype(i32), dst[t].astype(i32)]
    return tiles, experts, tuple(bundles)


def _tile(s, pref):
    return min(pref, s)


def _ssd_layer(x, sc, sh, gate, lng, lnb, w_in, conv_w, conv_b, dt_bias, a_log, d_skip, norm_w, w_out):
    s = x.shape[1]
    n_heads = dt_bias.shape[0]
    d_inner = n_heads * SSD_HEAD_DIM
    n_zxbc = w_in.shape[1] - n_heads
    pad = LANES - n_heads
    w_zxbc = w_in[:, :n_zxbc].astype(BF16)
    w_dt = jnp.pad(w_in[:, n_zxbc:], ((0, 0), (0, pad))).astype(BF16)
    zxbc, dt_raw = _ssd_in_proj(x, sc, sh, w_zxbc, w_dt, tm=_tile(s, 1024), tn=d_inner)
    yg = _ssd_scan(
        zxbc, dt_raw, conv_w, conv_b[None, :],
        jnp.pad(dt_bias, (0, pad))[None, :], jnp.pad(a_log, (0, pad))[None, :],
        jnp.repeat(d_skip, SSD_HEAD_DIM)[None, :], norm_w[None, :])
    return _mm_res_ln(yg, w_out.astype(BF16), x, gate, lng, lnb, tm=_tile(s, 512), name="ssd_out_proj")


def _mla_layer(x, positions, sc, sh, gate, lng, lnb, w_in, q_norm, kv_norm, w_uq, w_ukv, w_out):
    s = x.shape[1]
    d = x.shape[2]
    nh, hp = MLA_N_HEADS, MLA_HEAD_PAD
    qk = MLA_NOPE + MLA_ROPE
    rope_lo = MLA_NOPE
    w_cq_ckv = w_in[:, :MLA_Q_RANK + MLA_KV_RANK]
    w_kr = jnp.pad(w_in[:, MLA_Q_RANK + MLA_KV_RANK:], ((0, 0), (rope_lo, hp - qk)))
    w_in_pad = jnp.concatenate([w_cq_ckv, w_kr], axis=1).astype(BF16)
    cin = _mod_matmul(x, sc, sh, w_in_pad, tm=_tile(s, 1024), tn=MLA_IN_PAD, out_dtype=F32, name="mla_in_proj")
    w_uq_pad = jnp.pad(w_uq.reshape(MLA_Q_RANK, nh, qk), ((0, 0), (0, 0), (0, hp - qk)))
    w_uq_pad = w_uq_pad.reshape(MLA_Q_RANK, nh * hp).astype(BF16)
    w_ukv3 = w_ukv.reshape(MLA_KV_RANK, nh, MLA_NOPE + MLA_V)
    w_uk_pad = jnp.pad(w_ukv3[:, :, :MLA_NOPE], ((0, 0), (0, 0), (0, hp - MLA_NOPE)))
    w_uk_pad = w_uk_pad.reshape(MLA_KV_RANK, nh * hp).astype(BF16)
    w_uv = jnp.pad(w_ukv3[:, :, MLA_NOPE:], ((0, 0), (0, 0), (0, hp - MLA_V)))
    w_uv = w_uv.reshape(MLA_KV_RANK, nh * hp).astype(BF16)
    half = MLA_ROPE // 2
    freqs = ROPE_THETA ** (-jnp.arange(half, dtype=F32) / half)
    zeros = lambda n: jnp.zeros((n,), F32)
    freq_row = jnp.concatenate([zeros(rope_lo), freqs, freqs, zeros(hp - qk)])[None, :]
    sign_row = jnp.concatenate([zeros(rope_lo), -jnp.ones((half,), F32), jnp.ones((half,), F32),
                                zeros(hp - qk)])[None, :]
    q, k, v = _mla_proj(cin, positions[:, :, None], freq_row, sign_row, q_norm[None, :], kv_norm[None, :],
                        w_uq_pad, w_uk_pad, w_uv, tm=_tile(s, 512))
    attn = _attention(q, k, v, tq=_tile(s, 512))
    return _mm_res_ln(attn, w_out.astype(BF16), x, gate, lng, lnb, tm=_tile(s, 512), name="mla_out_proj")


def _sg_layer(x, sc, sh, gate, lng, lnb, w_in, b_in, ln_g, ln_b, w_s, b_s, w_out):
    s = x.shape[1]
    uv = _sg_in_proj(x, sc, sh, w_in.astype(BF16), b_in[None, :], ln_g[None, :], ln_b[None, :], tm=_tile(s, 512))
    return _sg_out(uv, w_s, b_s.T, w_out.astype(BF16), x, gate, lng, lnb, tm=_tile(s, 512))


def _dense_ffn(x, sc, sh, gate, lng, lnb, w_gate, w_up, w_down):
    s = x.shape[1]
    return _swiglu_res_ln(x, sc, sh, w_gate.astype(BF16), w_up.astype(BF16), w_down.astype(BF16),
                          gate, lng, lnb, tm=_tile(s, 512))


def _moe_ffn(x, sc, sh, gate, lng, lnb, w_router, w_gate_all, w_up_all, w_down_all, layer):
    b, s, d = x.shape
    ne = w_router.shape[1]
    ff = w_gate_all.shape[-1]
    bi = _tile(s, MOE_BLOCK)
    nblk = b * (s // bi)
    tokens = b * s
    c = MOE_COMBINE_TILES
    n_max = 2 * tokens // MOE_ROWS + nblk * ne
    n2_max = (n_max + ne * (MOE_GROUP - 1) + MOE_GROUP - 1) // MOE_GROUP
    n_bundles_max = (n_max + (c - 1) * nblk + c - 1) // c
    w_router_pad = jnp.pad(w_router, ((0, 0), (0, LANES - ne)))
    h, comb, pos, post, cnt = _router(x, sc, sh, w_router_pad, tm=bi)
    tiles, (exp2, tile2, n2), bundles = _moe_tile_lists(
        cnt[:, 0, :ne].astype(jnp.int32), n_max=n_max, n2_max=n2_max, n_bundles_max=n_bundles_max,
        rows=MOE_ROWS, group=MOE_GROUP)
    xg = _moe_dispatch(tiles, h.reshape(tokens, d), post, bi=bi, rows=MOE_ROWS, n_max=n_max,
                       n_rows=n2_max * MOE_GROUP * MOE_ROWS)
    y = _moe_experts(exp2 + layer * ne, tile2, n2, xg,
                     w_gate_all.astype(BF16).reshape(-1, d, ff), w_up_all.astype(BF16).reshape(-1, d, ff),
                     w_down_all.astype(BF16).reshape(-1, ff, d), rows2=MOE_GROUP * MOE_ROWS)
    out = _moe_combine(bundles, y, pos.reshape(tokens, LANES), comb.reshape(tokens, LANES), x.reshape(tokens, d),
                       gate, lng, lnb, bi=bi, rows=MOE_ROWS, blocks_per_batch=s // bi)
    return out.reshape(b, s, d)


def kernel(x, c, positions, ada_w, ada_b, ln_g, ln_b, ssd_w_in, ssd_conv_w, ssd_conv_b, ssd_dt_bias, ssd_a_log, ssd_d_skip, ssd_norm_w, ssd_w_out, mla_w_in, mla_q_norm, mla_kv_norm, mla_w_uq, mla_w_ukv, mla_w_out, sg_w_in, sg_b_in, sg_ln_g, sg_ln_b, sg_w_s, sg_b_s, sg_w_out, ffn_w_gate, ffn_w_up, ffn_w_down, moe_w_router, moe_w_gate, moe_w_up, moe_w_down):
    batch = x.shape[0]
    depth = ada_w.shape[0]
    c_pad = jnp.pad(c, ((0, SUBLANES - batch), (0, 0)))
    mod = _ada_mod(c_pad, ada_w, ada_b)[:, :, :batch]
    for i in range(depth):
        sh_m, sc_m, g_m, sh_f, sc_f, g_f = [mod[i, t][:, None, :] for t in range(6)]
        lng_m, lnb_m = ln_g[i, 0][None, :], ln_b[i, 0][None, :]
        lng_f, lnb_f = ln_g[i, 1][None, :], ln_b[i, 1][None, :]
        kind, j = i % 3, i // 3
        if kind == 0:
            x = _ssd_layer(x, sc_m, sh_m, g_m, lng_m, lnb_m, ssd_w_in[j], ssd_conv_w[j], ssd_conv_b[j],
                           ssd_dt_bias[j], ssd_a_log[j], ssd_d_skip[j], ssd_norm_w[j], ssd_w_out[j])
        elif kind == 1:
            x = _mla_layer(x, positions, sc_m, sh_m, g_m, lng_m, lnb_m, mla_w_in[j], mla_q_norm[j],
                           mla_kv_norm[j], mla_w_uq[j], mla_w_ukv[j], mla_w_out[j])
        else:
            x = _sg_layer(x, sc_m, sh_m, g_m, lng_m, lnb_m, sg_w_in[j], sg_b_in[j], sg_ln_g[j], sg_ln_b[j],
                          sg_w_s[j], sg_b_s[j], sg_w_out[j])
        k = i // 2
        if i % 2 == 0:
            x = _dense_ffn(x, sc_f, sh_f, g_f, lng_f, lnb_f, ffn_w_gate[k], ffn_w_up[k], ffn_w_down[k])
        else:
            x = _moe_ffn(x, sc_f, sh_f, g_f, lng_f, lnb_f, moe_w_router[k], moe_w_gate, moe_w_up, moe_w_down, k)
    return x
```
